```python
import math
import jax, jax.numpy as jnp
from jax import lax
import numpy as np

D_MODEL = 1024
BATCH = 16
SEQ = 4096
DEPTH = 1

CHUNK = 64
Q_BLOCK = 128
HEAD_DIM = 64
SB_HEADS = 8
SB_WIDTH = SB_HEADS * HEAD_DIM
DA_HEADS = 4
DA_V_DIM = 2 * HEAD_DIM
DA_QK_WIDTH = DA_HEADS * 2 * HEAD_DIM
DA_V_WIDTH = DA_HEADS * DA_V_DIM
N_BRANCH = 2
IN_SPLITS = (SB_WIDTH, SB_WIDTH, SB_WIDTH, DA_QK_WIDTH, DA_QK_WIDTH, DA_V_WIDTH, N_BRANCH * D_MODEL)
IN_WIDTH = sum(IN_SPLITS)
D_FF = -(-8 * D_MODEL // (3 * 256)) * 256
ROPE_THETA = 10000.0
NORM_EPS = 1e-6
SUBLN_EPS = 1e-5

kernel_name = "hybrid_stickbreak_diffattn_gated_block"


def rms_norm(x, g, eps=NORM_EPS):
    xf = x.astype(jnp.float32)
    y = xf * lax.rsqrt(jnp.mean(xf * xf, axis=-1, keepdims=True) + eps)
    return (y * g.astype(jnp.float32)).astype(x.dtype)


def rope_tables(seq_len):
    inv_freq = 1.0 / (ROPE_THETA ** (jnp.arange(0, HEAD_DIM, 2, dtype=jnp.float32) / HEAD_DIM))
    ang = jnp.arange(seq_len, dtype=jnp.float32)[:, None] * inv_freq[None, :]
    ang = jnp.concatenate([ang, ang], axis=-1)
    return jnp.cos(ang), jnp.sin(ang)


def apply_rope(x, cos, sin):
    half = HEAD_DIM // 2
    rot = jnp.concatenate([-x[..., half:], x[..., :half]], axis=-1)
    c = cos[None, :, None, :].astype(x.dtype)
    s = sin[None, :, None, :].astype(x.dtype)
    return x * c + rot * s


def stick_breaking_attention(q, k, v):
    seq_len = q.shape[2]
    scale = HEAD_DIM ** -0.5
    outs = []
    for i in range(seq_len // Q_BLOCK):
        start, end = i * Q_BLOCK, (i + 1) * Q_BLOCK
        z = jnp.einsum('bhqd,bhkd->bhqk', q[:, :, start:end], k[:, :, :end]).astype(jnp.float32) * scale
        qpos = jnp.arange(start, end)[:, None]
        kpos = jnp.arange(end)[None, :]
        mask = kpos < qpos
        log_keep = jnp.where(mask, jax.nn.log_sigmoid(-z), 0.0)
        log_between = lax.cumsum(log_keep, axis=3, reverse=True) - log_keep
        w = jnp.where(mask, jnp.exp(jax.nn.log_sigmoid(z) + log_between), 0.0)
        outs.append(jnp.einsum('bhqk,bhkd->bhqd', w.astype(v.dtype), v[:, :, :end]))
    return jnp.concatenate(outs, axis=2)


def differential_attention(q, k, v, lam):
    seq_len = q.shape[3]
    scale = HEAD_DIM ** -0.5
    outs = []
    for i in range(seq_len // Q_BLOCK):
        start, end = i * Q_BLOCK, (i + 1) * Q_BLOCK
        s = jnp.einsum('bhcqd,bhckd->bhcqk', q[:, :, :, start:end], k[:, :, :, :end]).astype(jnp.float32) * scale
        qchunk = jnp.arange(start, end)[:, None] // CHUNK
        kchunk = jnp.arange(end)[None, :] // CHUNK
        s = jnp.where(kchunk <= qchunk, s, -jnp.inf)
        p = jax.nn.softmax(s, axis=-1)
        a = p[:, :, 0] - lam * p[:, :, 1]
        outs.append(jnp.einsum('bhqk,bhkv->bhqv', a.astype(v.dtype), v[:, :, :end]))
    return jnp.concatenate(outs, axis=2)


def setup_inputs(seed: int = 0) -> dict:
    key = jax.random.key(seed)
    ks = jax.random.split(key, 16)
    nrm = jax.random.normal
    f32 = jnp.float32
    return {
        "x": nrm(ks[0], (BATCH, SEQ, D_MODEL), f32),
        "g_mix": 1.0 + 0.02 * nrm(ks[1], (DEPTH, D_MODEL), f32),
        "w_in": nrm(ks[2], (DEPTH, D_MODEL, IN_WIDTH), f32) * D_MODEL ** -0.5,
        "lambda_q1": 0.1 * nrm(ks[3], (DEPTH, HEAD_DIM), f32),
        "lambda_k1": 0.1 * nrm(ks[4], (DEPTH, HEAD_DIM), f32),
        "lambda_q2": 0.1 * nrm(ks[5], (DEPTH, HEAD_DIM), f32),
        "lambda_k2": 0.1 * nrm(ks[6], (DEPTH, HEAD_DIM), f32),
        "g_subln": 1.0 + 0.02 * nrm(ks[7], (DEPTH, DA_V_DIM), f32),
        "w_branch_sb": nrm(ks[8], (DEPTH, SB_WIDTH, D_MODEL), f32) * SB_WIDTH ** -0.5,
        "w_branch_da": nrm(ks[9], (DEPTH, DA_V_WIDTH, D_MODEL), f32) * DA_V_WIDTH ** -0.5,
        "w_out": nrm(ks[10], (DEPTH, D_MODEL, D_MODEL), f32) * D_MODEL ** -0.5,
        "g_ffn": 1.0 + 0.02 * nrm(ks[11], (DEPTH, D_MODEL), f32),
        "w_ffn_gate": nrm(ks[12], (DEPTH, D_MODEL, D_FF), f32) * D_MODEL ** -0.5,
        "w_ffn_up": nrm(ks[13], (DEPTH, D_MODEL, D_FF), f32) * D_MODEL ** -0.5,
        "w_ffn_down": nrm(ks[14], (DEPTH, D_FF, D_MODEL), f32) * D_FF ** -0.5,
        "g_final": 1.0 + 0.02 * nrm(ks[15], (D_MODEL,), f32),
    }


def reference(x, g_mix, w_in, lambda_q1, lambda_k1, lambda_q2, lambda_k2, g_subln,
              w_branch_sb, w_branch_da, w_out, g_ffn, w_ffn_gate, w_ffn_up, w_ffn_down, g_final):
    B, S, _ = x.shape
    cos, sin = rope_tables(S)
    offsets = [int(o) for o in np.cumsum(IN_SPLITS)[:-1]]
    for l in range(DEPTH):
        lambda_init = 0.8 - 0.6 * math.exp(-0.3 * l)
        h = rms_norm(x, g_mix[l])
        proj = h @ w_in[l]
        qa, ka, va, qd, kd, vd, gates = jnp.split(proj, offsets, axis=-1)
        qa = qa.reshape(B, S, SB_HEADS, HEAD_DIM).transpose(0, 2, 1, 3)
        ka = ka.reshape(B, S, SB_HEADS, HEAD_DIM).transpose(0, 2, 1, 3)
        va = va.reshape(B, S, SB_HEADS, HEAD_DIM).transpose(0, 2, 1, 3)
        o_sb = stick_breaking_attention(qa, ka, va).transpose(0, 2, 1, 3).reshape(B, S, SB_WIDTH)
        qd = apply_rope(qd.reshape(B, S, 2 * DA_HEADS, HEAD_DIM), cos, sin)
        kd = apply_rope(kd.reshape(B, S, 2 * DA_HEADS, HEAD_DIM), cos, sin)
        qd = qd.reshape(B, S, DA_HEADS, 2, HEAD_DIM).transpose(0, 2, 3, 1, 4)
        kd = kd.reshape(B, S, DA_HEADS, 2, HEAD_DIM).transpose(0, 2, 3, 1, 4)
        vd = vd.reshape(B, S, DA_HEADS, DA_V_DIM).transpose(0, 2, 1, 3)
        lam = (jnp.exp(jnp.sum(lambda_q1[l].astype(jnp.float32) * lambda_k1[l].astype(jnp.float32)))
               - jnp.exp(jnp.sum(lambda_q2[l].astype(jnp.float32) * lambda_k2[l].astype(jnp.float32)))
               + lambda_init)
        o_da = differential_attention(qd, kd, vd, lam)
        o_da = rms_norm(o_da, g_subln[l], SUBLN_EPS) * (1.0 - lambda_init)
        o_da = o_da.transpose(0, 2, 1, 3).reshape(B, S, DA_V_WIDTH)
        gate_sb, gate_da = jnp.split(gates, N_BRANCH, axis=-1)
        mixed = (jax.nn.sigmoid(gate_sb) * (o_sb @ w_branch_sb[l])
                 + jax.nn.sigmoid(gate_da) * (o_da @ w_branch_da[l]))
        x = x + mixed @ w_out[l]
        h = rms_norm(x, g_ffn[l])
        x = x + (jax.nn.silu(h @ w_ffn_gate[l]) * (h @ w_ffn_up[l])) @ w_ffn_down[l]
    return rms_norm(x, g_final)
```

```python
import functools
import math

import jax
import jax.numpy as jnp
from jax import lax
from jax.experimental import pallas as pl
from jax.experimental.pallas import tpu as pltpu

D_MODEL = 1024
HEAD_DIM = 64
CHUNK = 64
SB_WIDTH = 512
DA_QK_WIDTH = 512
DA_V_WIDTH = 512
DA_HEADS = 4
N_BRANCH = 2
IN_WIDTH = 3 * SB_WIDTH + 2 * DA_QK_WIDTH + DA_V_WIDTH + N_BRANCH * D_MODEL
D_FF = 2816
ROPE_THETA = 10000.0
NORM_EPS = 1e-6
SUBLN_EPS = 1e-5
QK_SCALE = HEAD_DIM ** -0.5

LANES = 128
COL_TILE = 512
ATTN_TILE = 256
VMEM_LIMIT = 56 * 1024 * 1024

_QA, _KA, _VA = 0, 4, 8
_QD, _KD, _VD = 12, 16, 20
_GATE = 3

F32 = jnp.float32
BF16 = jnp.bfloat16


def _rms(x, g, eps):
    return x * lax.rsqrt(jnp.mean(x * x, axis=-1, keepdims=True) + eps) * g


def _nt_dot(a, b):
    return lax.dot_general(a, b, (((1,), (1,)), ((), ())), preferred_element_type=F32)


def _dot(a, b):
    return jnp.dot(a, b, preferred_element_type=F32)


def _in_proj_kernel(x_ref, g_ref, w_ref, cos_ref, sin_ref, o_ref):
    h = _rms(x_ref[...], g_ref[...], NORM_EPS).astype(BF16)
    tm = h.shape[0]
    lane = lax.broadcasted_iota(jnp.int32, (tm, LANES), 1)
    first_half = (lane % HEAD_DIM) < (HEAD_DIM // 2)
    cos = cos_ref[...]
    sin = sin_ref[...]

    def rope(a):
        outs = []
        for c in range(a.shape[1] // LANES):
            xs = a[:, c * LANES:(c + 1) * LANES]
            rot = jnp.where(first_half, pltpu.roll(xs, LANES - HEAD_DIM // 2, 1),
                            pltpu.roll(xs, HEAD_DIM // 2, 1))
            outs.append(xs * cos + rot * sin)
        return jnp.concatenate(outs, axis=1)

    for j in range(IN_WIDTH // COL_TILE):
        cols = slice(j * COL_TILE, (j + 1) * COL_TILE)
        a = _dot(h, w_ref[:, cols])
        if j == 0:
            a = a * QK_SCALE
        elif j == 3:
            a = rope(a) * QK_SCALE
        elif j == 4:
            a = rope(a)
        elif j >= 6:
            a = jax.nn.sigmoid(a)
        o_ref[:, cols] = a.astype(o_ref.dtype)


def _in_proj(x2, g, w, cos, sin, seq, tm):
    t = x2.shape[0]
    pos_blocks = seq // tm
    return pl.pallas_call(
        _in_proj_kernel,
        grid=(t // tm,),
        in_specs=[
            pl.BlockSpec((tm, D_MODEL), lambda i: (i, 0)),
            pl.BlockSpec((1, D_MODEL), lambda i: (0, 0)),
            pl.BlockSpec((D_MODEL, IN_WIDTH), lambda i: (0, 0)),
            pl.BlockSpec((tm, LANES), lambda i: (i % pos_blocks, 0)),
            pl.BlockSpec((tm, LANES), lambda i: (i % pos_blocks, 0)),
        ],
        out_specs=pl.BlockSpec((tm, IN_WIDTH), lambda i: (i, 0)),
        out_shape=jax.ShapeDtypeStruct((t, IN_WIDTH), BF16),
        compiler_params=pltpu.CompilerParams(
            dimension_semantics=("arbitrary",), vmem_limit_bytes=VMEM_LIMIT),
        name="in_proj",
    )(x2, g, w, cos, sin)


def _sb_kernel(q_ref, k_ref, v_ref, o_ref, acc_ref, r_ref):
    tq = q_ref.shape[1]
    qi = pl.program_id(2)
    q = q_ref[0]
    lane = lax.broadcasted_iota(jnp.int32, (tq, LANES), 1)
    qh = (jnp.where(lane < HEAD_DIM, q, jnp.zeros_like(q)),
          jnp.where(lane >= HEAD_DIM, q, jnp.zeros_like(q)))
    row = lax.broadcasted_iota(jnp.int32, (tq, tq), 0)
    col = lax.broadcasted_iota(jnp.int32, (tq, tq), 1)
    suffix = jnp.where(row >= col, 1.0, 0.0).astype(BF16)
    earlier = col < row
    acc_ref[...] = jnp.zeros_like(acc_ref)
    r_ref[...] = jnp.zeros_like(r_ref)

    def tile(j, masked):
        start = pl.multiple_of(j * tq, tq)
        k = k_ref[0, pl.ds(start, tq), :]
        v = v_ref[0, pl.ds(start, tq), :]
        for h in range(2):
            z = _nt_dot(qh[h], k)
            log_keep = jnp.minimum(-z, 0.0) - jnp.log(1.0 + jnp.exp(-jnp.abs(z)))
            if masked:
                log_keep = jnp.where(earlier, log_keep, 0.0)
            hi = log_keep.astype(BF16)
            lo = (log_keep - hi.astype(F32)).astype(BF16)
            r = r_ref[h]
            e = z + (_dot(hi, suffix) + _dot(lo, suffix)) + jnp.concatenate([r] * (tq // LANES), axis=1)
            w = jnp.exp(e)
            if masked:
                w = jnp.where(earlier, w, 0.0)
            acc_ref[h] += _dot(w.astype(BF16), v)
            r_ref[h] = r + jnp.sum(log_keep, axis=1, keepdims=True)

    tile(qi, True)

    def body(jj, carry):
        tile(qi - 1 - jj, False)
        return carry

    lax.fori_loop(0, qi, body, 0)
    o_ref[0] = jnp.where(lane < HEAD_DIM, acc_ref[0], acc_ref[1]).astype(o_ref.dtype)


def _sb_attention(proj3, tq):
    b, s, _ = proj3.shape
    return pl.pallas_call(
        _sb_kernel,
        grid=(b, SB_WIDTH // LANES, s // tq),
        in_specs=[
            pl.BlockSpec((1, tq, LANES), lambda bi, p, qi: (bi, qi, _QA + p)),
            pl.BlockSpec((1, s, LANES), lambda bi, p, qi: (bi, 0, _KA + p)),
            pl.BlockSpec((1, s, LANES), lambda bi, p, qi: (bi, 0, _VA + p)),
        ],
        out_specs=pl.BlockSpec((1, tq, LANES), lambda bi, p, qi: (bi, qi, p)),
        out_shape=jax.ShapeDtypeStruct((b, s, SB_WIDTH), BF16),
        scratch_shapes=[pltpu.VMEM((2, tq, LANES), F32), pltpu.VMEM((2, tq, LANES), F32)],
        compiler_params=pltpu.CompilerParams(
            dimension_semantics=("arbitrary", "arbitrary", "arbitrary"), vmem_limit_bytes=VMEM_LIMIT),
        name="sb_attn",
    )(proj3, proj3, proj3)


def _da_kernel(lam_ref, g_ref, q_ref, k_ref, v_ref, o_ref, acc_ref, m_ref, l_ref, *, lambda_init):
    tq = q_ref.shape[1]
    qi = pl.program_id(2)
    q = q_ref[0]
    lane = lax.broadcasted_iota(jnp.int32, (tq, LANES), 1)
    qh = (jnp.where(lane < HEAD_DIM, q, jnp.zeros_like(q)),
          jnp.where(lane >= HEAD_DIM, q, jnp.zeros_like(q)))
    row = lax.broadcasted_iota(jnp.int32, (tq, tq), 0)
    col = lax.broadcasted_iota(jnp.int32, (tq, tq), 1)
    visible = (col // CHUNK) <= (row // CHUNK)
    acc_ref[...] = jnp.zeros_like(acc_ref)
    l_ref[...] = jnp.zeros_like(l_ref)
    m_ref[...] = jnp.full_like(m_ref, -jnp.inf)

    def tile(j, masked):
        start = pl.multiple_of(j * tq, tq)
        k = k_ref[0, pl.ds(start, tq), :]
        v = v_ref[0, pl.ds(start, tq), :]
        for c in range(2):
            s = _nt_dot(qh[c], k)
            if masked:
                s = jnp.where(visible, s, -jnp.inf)
            m_prev = m_ref[c]
            m_new = jnp.maximum(m_prev, jnp.max(s, axis=1, keepdims=True))
            alpha = jnp.exp(m_prev - m_new)
            p = jnp.exp(s - jnp.concatenate([m_new] * (tq // LANES), axis=1))
            l_ref[c] = alpha * l_ref[c] + jnp.sum(p, axis=1, keepdims=True)
            acc_ref[c] = alpha * acc_ref[c] + _dot(p.astype(BF16), v)
            m_ref[c] = m_new

    tile(qi, True)

    def body(j, carry):
        tile(j, False)
        return carry

    lax.fori_loop(0, qi, body, 0)

    lam_p = lam_ref[...]
    lam = (jnp.exp(jnp.sum(lam_p[0:1] * lam_p[1:2], axis=1, keepdims=True))
           - jnp.exp(jnp.sum(lam_p[2:3] * lam_p[3:4], axis=1, keepdims=True)) + lambda_init)
    o = acc_ref[0] / l_ref[0] - lam * (acc_ref[1] / l_ref[1])
    o_ref[0] = (_rms(o, g_ref[...], SUBLN_EPS) * (1.0 - lambda_init)).astype(o_ref.dtype)


def _da_attention(proj3, lam_params, g_subln, lambda_init, tq):
    b, s, _ = proj3.shape
    return pl.pallas_call(
        functools.partial(_da_kernel, lambda_init=lambda_init),
        grid=(b, DA_HEADS, s // tq),
        in_specs=[
            pl.BlockSpec((4, HEAD_DIM), lambda bi, h, qi: (0, 0)),
            pl.BlockSpec((1, LANES), lambda bi, h, qi: (0, 0)),
            pl.BlockSpec((1, tq, LANES), lambda bi, h, qi: (bi, qi, _QD + h)),
            pl.BlockSpec((1, s, LANES), lambda bi, h, qi: (bi, 0, _KD + h)),
            pl.BlockSpec((1, s, LANES), lambda bi, h, qi: (bi, 0, _VD + h)),
        ],
        out_specs=pl.BlockSpec((1, tq, LANES), lambda bi, h, qi: (bi, qi, h)),
        out_shape=jax.ShapeDtypeStruct((b, s, DA_V_WIDTH), BF16),
        scratch_shapes=[pltpu.VMEM((2, tq, LANES), F32)] * 3,
        compiler_params=pltpu.CompilerParams(
            dimension_semantics=("arbitrary", "arbitrary", "arbitrary"), vmem_limit_bytes=VMEM_LIMIT),
        name="da_attn",
    )(lam_params, g_subln, proj3, proj3, proj3)


def _merge_kernel(x_ref, osb_ref, oda_ref, gsb_ref, gda_ref, wsb_ref, wda_ref, wout_ref, o_ref):
    mixed = (gsb_ref[...].astype(F32) * _dot(osb_ref[...], wsb_ref[...])
             + gda_ref[...].astype(F32) * _dot(oda_ref[...], wda_ref[...]))
    o_ref[...] = x_ref[...] + _dot(mixed.astype(BF16), wout_ref[...])


def _merge(x2, o_sb, o_da, proj, w_sb, w_da, w_out, tm):
    t = x2.shape[0]
    const = lambda i: (0, 0)
    return pl.pallas_call(
        _merge_kernel,
        grid=(t // tm,),
        in_specs=[
            pl.BlockSpec((tm, D_MODEL), lambda i: (i, 0)),
            pl.BlockSpec((tm, SB_WIDTH), lambda i: (i, 0)),
            pl.BlockSpec((tm, DA_V_WIDTH), lambda i: (i, 0)),
            pl.BlockSpec((tm, D_MODEL), lambda i: (i, _GATE)),
            pl.BlockSpec((tm, D_MODEL), lambda i: (i, _GATE + 1)),
            pl.BlockSpec((SB_WIDTH, D_MODEL), const),
            pl.BlockSpec((DA_V_WIDTH, D_MODEL), const),
            pl.BlockSpec((D_MODEL, D_MODEL), const),
        ],
        out_specs=pl.BlockSpec((tm, D_MODEL), lambda i: (i, 0)),
        out_shape=jax.ShapeDtypeStruct((t, D_MODEL), F32),
        compiler_params=pltpu.CompilerParams(
            dimension_semantics=("arbitrary",), vmem_limit_bytes=VMEM_LIMIT),
        name="merge",
    )(x2, o_sb, o_da, proj, proj, w_sb, w_da, w_out)


def _ffn_kernel(x_ref, g_ref, wg_ref, wu_ref, wd_ref, gf_ref, o_ref, *, final_norm):
    x = x_ref[...]
    h = _rms(x, g_ref[...], NORM_EPS).astype(BF16)
    a = (jax.nn.silu(_dot(h, wg_ref[...])) * _dot(h, wu_ref[...])).astype(BF16)
    y = x + _dot(a, wd_ref[...])
    if final_norm:
        y = _rms(y, gf_ref[...], NORM_EPS)
    o_ref[...] = y


def _ffn(x2, g, w_gate, w_up, w_down, g_final, final_norm, tm):
    t = x2.shape[0]
    const = lambda i: (0, 0)
    return pl.pallas_call(
        functools.partial(_ffn_kernel, final_norm=final_norm),
        grid=(t // tm,),
        in_specs=[
            pl.BlockSpec((tm, D_MODEL), lambda i: (i, 0)),
            pl.BlockSpec((1, D_MODEL), const),
            pl.BlockSpec((D_MODEL, D_FF), const),
            pl.BlockSpec((D_MODEL, D_FF), const),
            pl.BlockSpec((D_FF, D_MODEL), const),
            pl.BlockSpec((1, D_MODEL), const),
        ],
        out_specs=pl.BlockSpec((tm, D_MODEL), lambda i: (i, 0)),
        out_shape=jax.ShapeDtypeStruct((t, D_MODEL), F32),
        compiler_params=pltpu.CompilerParams(
            dimension_semantics=("arbitrary",), vmem_limit_bytes=VMEM_LIMIT),
        name="ffn",
    )(x2, g, w_gate, w_up, w_down, g_final)


def _rope_tables(seq_len):
    inv_freq = 1.0 / (ROPE_THETA ** (jnp.arange(0, HEAD_DIM, 2, dtype=F32) / HEAD_DIM))
    ang = jnp.arange(seq_len, dtype=F32)[:, None] * inv_freq[None, :]
    ang = jnp.concatenate([ang, ang], axis=-1)
    sign = jnp.where(jnp.arange(HEAD_DIM) < HEAD_DIM // 2, -1.0, 1.0).astype(F32)
    cos = jnp.tile(jnp.cos(ang), (1, LANES // HEAD_DIM))
    sin = jnp.tile(jnp.sin(ang) * sign[None, :], (1, LANES // HEAD_DIM))
    return cos, sin


def kernel(x, g_mix, w_in, lambda_q1, lambda_k1, lambda_q2, lambda_k2, g_subln, w_branch_sb, w_branch_da,
           w_out, g_ffn, w_ffn_gate, w_ffn_up, w_ffn_down, g_final):
    b, s, d = x.shape
    depth = w_in.shape[0]
    assert d == D_MODEL and s % ATTN_TILE == 0
    tm = min(256, s)
    cos, sin = _rope_tables(s)
    x2 = x.reshape(b * s, d)
    for l in range(depth):
        lambda_init = 0.8 - 0.6 * math.exp(-0.3 * l)
        proj = _in_proj(x2, g_mix[l][None], w_in[l].astype(BF16), cos, sin, s, tm)
        proj3 = proj.reshape(b, s, IN_WIDTH)
        o_sb = _sb_attention(proj3, ATTN_TILE)
        lam_params = jnp.stack([lambda_q1[l], lambda_k1[l], lambda_q2[l], lambda_k2[l]]).astype(F32)
        o_da = _da_attention(proj3, lam_params, g_subln[l][None], lambda_init, ATTN_TILE)
        x2 = _merge(x2, o_sb.reshape(b * s, SB_WIDTH), o_da.reshape(b * s, DA_V_WIDTH), proj,
                    w_branch_sb[l].astype(BF16), w_branch_da[l].astype(BF16), w_out[l].astype(BF16), tm)
        x2 = _ffn(x2, g_ffn[l][None], w_ffn_gate[l].astype(BF16), w_ffn_up[l].astype(BF16),
                  w_ffn_down[l].astype(BF16), g_final[None], l == depth - 1, tm)
    return x2.reshape(b, s, d)
```

```python
import functools
import math

import jax
import jax.numpy as jnp
from jax import lax
from jax.experimental import pallas as pl
from jax.experimental.pallas import tpu as pltpu

D_MODEL = 1024
HEAD_DIM = 64
CHUNK = 64
SB_WIDTH = 512
DA_QK_WIDTH = 512
DA_V_WIDTH = 512
DA_HEADS = 4
N_BRANCH = 2
IN_WIDTH = 3 * SB_WIDTH + 2 * DA_QK_WIDTH + DA_V_WIDTH + N_BRANCH * D_MODEL
D_FF = 2816
ROPE_THETA = 10000.0
NORM_EPS = 1e-6
SUBLN_EPS = 1e-5
QK_SCALE = HEAD_DIM ** -0.5
LOG2E = math.log2(math.e)

LANES = 128
COL_TILE = 512
ATTN_TILE = 256
UNROLL = 4
VMEM_LIMIT = 56 * 1024 * 1024

_QA, _KA, _VA = 0, 4, 8
_QD, _KD, _VD = 12, 16, 20
_GATE = 3

F32 = jnp.float32
BF16 = jnp.bfloat16


def _rms(x, g, eps):
    return x * lax.rsqrt(jnp.mean(x * x, axis=-1, keepdims=True) + eps) * g


def _nt_dot(a, b):
    return lax.dot_general(a, b, (((1,), (1,)), ((), ())), preferred_element_type=F32)


def _dot(a, b):
    return jnp.dot(a, b, preferred_element_type=F32)


def _in_proj_kernel(x_ref, g_ref, w_ref, cos_ref, sin_ref, o_ref):
    h = _rms(x_ref[...], g_ref[...], NORM_EPS).astype(BF16)
    tm = h.shape[0]
    lane = lax.broadcasted_iota(jnp.int32, (tm, LANES), 1)
    first_half = (lane % HEAD_DIM) < (HEAD_DIM // 2)
    cos = cos_ref[...]
    sin = sin_ref[...]

    def rope(a):
        outs = []
        for c in range(a.shape[1] // LANES):
            xs = a[:, c * LANES:(c + 1) * LANES]
            rot = jnp.where(first_half, pltpu.roll(xs, LANES - HEAD_DIM // 2, 1),
                            pltpu.roll(xs, HEAD_DIM // 2, 1))
            outs.append(xs * cos + rot * sin)
        return jnp.concatenate(outs, axis=1)

    for j in range(IN_WIDTH // COL_TILE):
        cols = slice(j * COL_TILE, (j + 1) * COL_TILE)
        a = _dot(h, w_ref[:, cols])
        if j == 0:
            a = a * (QK_SCALE * LOG2E)
        elif j == 3:
            a = rope(a) * QK_SCALE
        elif j == 4:
            a = rope(a)
        elif j >= 6:
            a = jax.nn.sigmoid(a)
        o_ref[:, cols] = a.astype(o_ref.dtype)


def _in_proj(x2, g, w, cos, sin, seq, tm):
    t = x2.shape[0]
    pos_blocks = seq // tm
    return pl.pallas_call(
        _in_proj_kernel,
        grid=(t // tm,),
        in_specs=[
            pl.BlockSpec((tm, D_MODEL), lambda i: (i, 0)),
            pl.BlockSpec((1, D_MODEL), lambda i: (0, 0)),
            pl.BlockSpec((D_MODEL, IN_WIDTH), lambda i: (0, 0)),
            pl.BlockSpec((tm, LANES), lambda i: (i % pos_blocks, 0)),
            pl.BlockSpec((tm, LANES), lambda i: (i % pos_blocks, 0)),
        ],
        out_specs=pl.BlockSpec((tm, IN_WIDTH), lambda i: (i, 0)),
        out_shape=jax.ShapeDtypeStruct((t, IN_WIDTH), BF16),
        compiler_params=pltpu.CompilerParams(
            dimension_semantics=("arbitrary",), vmem_limit_bytes=VMEM_LIMIT),
        name="in_proj",
    )(x2, g, w, cos, sin)


def _tile_schedule(n_tiles):
    steps = [(i, i) for i in range(n_tiles)]
    steps += [(i, j) for i in range(n_tiles) for j in range(i - 1, -1, -1)]
    return steps


def _sb_kernel(qi_tab, kj_tab, q_ref, k_ref, v_ref, o_ref, qs, vm, u2, racc, oacc,
               z0, z1, hl0, hl1, rs0, rs1, w0, w1, sc0, sc1, *, n_steps, n_diag):
    t = ATTN_TILE
    s_len = q_ref.shape[1]
    lane = lax.broadcasted_iota(jnp.int32, (t, LANES), 1)
    head0 = lane < HEAD_DIM

    def prep(c, carry):
        rows = pl.ds(pl.multiple_of(c * t, t), t)
        qc = q_ref[0, rows, :]
        vc = v_ref[0, rows, :]
        zero = jnp.zeros_like(qc)
        qs[0, rows, :] = jnp.where(head0, qc, zero)
        qs[1, rows, :] = jnp.where(head0, zero, qc)
        vm[0, rows, :] = jnp.where(head0, vc, zero)
        vm[1, rows, :] = jnp.where(head0, zero, vc)
        oacc[rows, :] = jnp.zeros((t, LANES), F32)
        racc[0, rows, :] = jnp.zeros((t, LANES), F32)
        racc[1, rows, :] = jnp.zeros((t, LANES), F32)
        return carry

    lax.fori_loop(0, s_len // t, prep, 0)
    row = lax.broadcasted_iota(jnp.int32, (t, t), 0)
    col = lax.broadcasted_iota(jnp.int32, (t, t), 1)
    suffix = jnp.where(row >= col, 1.0, 0.0).astype(BF16)
    u2[0:t, :] = suffix
    u2[t:2 * t, :] = suffix

    def rows_of(tab, s):
        return pl.ds(pl.multiple_of(tab[s] * t, t), t)

    def stage_a(s, z_s, hl, rs_s, masked):
        qrows = rows_of(qi_tab, s)
        q2 = jnp.concatenate([qs[0, qrows, :], qs[1, qrows, :]], axis=0)
        z = _nt_dot(q2, k_ref[0, rows_of(kj_tab, s), :])
        neg_abs = lax.bitcast_convert_type(
            lax.bitcast_convert_type(z, jnp.uint32) | jnp.uint32(0x80000000), F32)
        sp = jnp.maximum(z, 0.0) + jnp.log2(1.0 + jnp.exp2(neg_abs))
        if masked:
            r2 = lax.broadcasted_iota(jnp.int32, (2 * t, t), 0)
            c2 = lax.broadcasted_iota(jnp.int32, (2 * t, t), 1)
            earlier = c2 < (r2 % t)
            sp = jnp.where(earlier, sp, 0.0)
            z = jnp.where(earlier, z, -jnp.inf)
        z_s[...] = z
        hi = sp.astype(BF16)
        hl[:, 0:t] = hi
        hl[:, t:2 * t] = (sp - hi.astype(F32)).astype(BF16)
        rs_s[...] = jnp.broadcast_to(jnp.sum(sp, axis=1, keepdims=True), (2 * t, LANES))

    def stage_b(s, z_s, hl, rs_s, w, sc):
        wv = jnp.exp2(z_s[...] - _dot(hl[...], u2[...])).astype(BF16)
        w[:, 0:t] = wv[:t]
        w[:, t:2 * t] = wv[t:]
        qrows = rows_of(qi_tab, s)
        rs = rs_s[...]
        r0 = racc[0, qrows, :]
        r1 = racc[1, qrows, :]
        racc[0, qrows, :] = r0 + rs[:t]
        racc[1, qrows, :] = r1 + rs[t:]
        sc[...] = jnp.exp2(-jnp.where(head0, r0, r1))

    def stage_c(s, w, sc):
        krows = rows_of(kj_tab, s)
        v2 = jnp.concatenate([vm[0, krows, :], vm[1, krows, :]], axis=0)
        oacc[rows_of(qi_tab, s), :] += _dot(w[...], v2) * sc[...]

    slot = ((z0, hl0, rs0, w0, sc0), (z1, hl1, rs1, w1, sc1))

    def step(s, parity, masked):
        z_a, hl_a, rs_a, w_c, sc_c = slot[parity]
        z_b, hl_b, rs_b, w_b, sc_b = slot[1 - parity]
        stage_a(s, z_a, hl_a, rs_a, masked)
        stage_b(s - 1, z_b, hl_b, rs_b, w_b, sc_b)
        stage_c(s - 2, w_c, sc_c)

    def steps_per_trip(n, masked):
        def body(i, carry):
            for u in range(n):
                step(n * i + u, u % 2, masked)
            return carry
        return body

    stage_a(0, z0, hl0, rs0, True)
    stage_a(1, z1, hl1, rs1, True)
    stage_b(0, z0, hl0, rs0, w0, sc0)
    lax.fori_loop(1, n_diag // 2, steps_per_trip(2, True), 0)
    lax.fori_loop(n_diag // UNROLL, n_steps // UNROLL, steps_per_trip(UNROLL, False), 0)
    stage_b(n_steps - 1, z1, hl1, rs1, w1, sc1)
    stage_c(n_steps - 2, w0, sc0)
    stage_c(n_steps - 1, w1, sc1)
    o_ref[0] = oacc[...].astype(o_ref.dtype)


def _sb_attention(proj3):
    b, s, _ = proj3.shape
    t = ATTN_TILE
    n_tiles = s // t
    steps = _tile_schedule(n_tiles)
    assert n_tiles % UNROLL == 0 and len(steps) % UNROLL == 0
    qi_tab = jnp.asarray([q for q, _ in steps], jnp.int32)
    kj_tab = jnp.asarray([k for _, k in steps], jnp.int32)
    col_block = lambda off: pl.BlockSpec((1, s, LANES), lambda bi, p, *_: (bi, 0, off + p))
    return pl.pallas_call(
        functools.partial(_sb_kernel, n_steps=len(steps), n_diag=n_tiles),
        grid_spec=pltpu.PrefetchScalarGridSpec(
            num_scalar_prefetch=2,
            grid=(b, SB_WIDTH // LANES),
            in_specs=[col_block(_QA), col_block(_KA), col_block(_VA)],
            out_specs=pl.BlockSpec((1, s, LANES), lambda bi, p, *_: (bi, 0, p)),
            scratch_shapes=[
                pltpu.VMEM((2, s, LANES), BF16),
                pltpu.VMEM((2, s, LANES), BF16),
                pltpu.VMEM((2 * t, t), BF16),
                pltpu.VMEM((2, s, LANES), F32),
                pltpu.VMEM((s, LANES), F32),
                pltpu.VMEM((2 * t, t), F32), pltpu.VMEM((2 * t, t), F32),
                pltpu.VMEM((2 * t, 2 * t), BF16), pltpu.VMEM((2 * t, 2 * t), BF16),
                pltpu.VMEM((2 * t, LANES), F32), pltpu.VMEM((2 * t, LANES), F32),
                pltpu.VMEM((t, 2 * t), BF16), pltpu.VMEM((t, 2 * t), BF16),
                pltpu.VMEM((t, LANES), F32), pltpu.VMEM((t, LANES), F32),
            ]),
        out_shape=jax.ShapeDtypeStruct((b, s, SB_WIDTH), BF16),
        compiler_params=pltpu.CompilerParams(
            dimension_semantics=("arbitrary", "arbitrary"), vmem_limit_bytes=VMEM_LIMIT),
        name="sb_attn",
    )(qi_tab, kj_tab, proj3, proj3, proj3)


def _da_kernel(lam_ref, g_ref, q_ref, k_ref, v_ref, o_ref, acc_ref, m_ref, l_ref, *, lambda_init):
    tq = q_ref.shape[1]
    qi = pl.program_id(2)
    q = q_ref[0]
    lane = lax.broadcasted_iota(jnp.int32, (tq, LANES), 1)
    qh = (jnp.where(lane < HEAD_DIM, q, jnp.zeros_like(q)),
          jnp.where(lane >= HEAD_DIM, q, jnp.zeros_like(q)))
    row = lax.broadcasted_iota(jnp.int32, (tq, tq), 0)
    col = lax.broadcasted_iota(jnp.int32, (tq, tq), 1)
    visible = (col // CHUNK) <= (row // CHUNK)
    acc_ref[...] = jnp.zeros_like(acc_ref)
    l_ref[...] = jnp.zeros_like(l_ref)
    m_ref[...] = jnp.full_like(m_ref, -jnp.inf)

    def tile(j, masked):
        start = pl.multiple_of(j * tq, tq)
        k = k_ref[0, pl.ds(start, tq), :]
        v = v_ref[0, pl.ds(start, tq), :]
        for c in range(2):
            s = _nt_dot(qh[c], k)
            if masked:
                s = jnp.where(visible, s, -jnp.inf)
            m_prev = m_ref[c]
            m_new = jnp.maximum(m_prev, jnp.max(s, axis=1, keepdims=True))
            alpha = jnp.exp(m_prev - m_new)
            p = jnp.exp(s - jnp.concatenate([m_new] * (tq // LANES), axis=1))
            l_ref[c] = alpha * l_ref[c] + jnp.sum(p, axis=1, keepdims=True)
            acc_ref[c] = alpha * acc_ref[c] + _dot(p.astype(BF16), v)
            m_ref[c] = m_new

    tile(qi, True)

    def body(j, carry):
        tile(j, False)
        return carry

    lax.fori_loop(0, qi, body, 0)

    lam_p = lam_ref[...]
    lam = (jnp.exp(jnp.sum(lam_p[0:1] * lam_p[1:2], axis=1, keepdims=True))
           - jnp.exp(jnp.sum(lam_p[2:3] * lam_p[3:4], axis=1, keepdims=True)) + lambda_init)
    o = acc_ref[0] / l_ref[0] - lam * (acc_ref[1] / l_ref[1])
    o_ref[0] = (_rms(o, g_ref[...], SUBLN_EPS) * (1.0 - lambda_init)).astype(o_ref.dtype)


def _da_attention(proj3, lam_params, g_subln, lambda_init, tq):
    b, s, _ = proj3.shape
    return pl.pallas_call(
        functools.partial(_da_kernel, lambda_init=lambda_init),
        grid=(b, DA_HEADS, s // tq),
        in_specs=[
            pl.BlockSpec((4, HEAD_DIM), lambda bi, h, qi: (0, 0)),
            pl.BlockSpec((1, LANES), lambda bi, h, qi: (0, 0)),
            pl.BlockSpec((1, tq, LANES), lambda bi, h, qi: (bi, qi, _QD + h)),
            pl.BlockSpec((1, s, LANES), lambda bi, h, qi: (bi, 0, _KD + h)),
            pl.BlockSpec((1, s, LANES), lambda bi, h, qi: (bi, 0, _VD + h)),
        ],
        out_specs=pl.BlockSpec((1, tq, LANES), lambda bi, h, qi: (bi, qi, h)),
        out_shape=jax.ShapeDtypeStruct((b, s, DA_V_WIDTH), BF16),
        scratch_shapes=[pltpu.VMEM((2, tq, LANES), F32)] * 3,
        compiler_params=pltpu.CompilerParams(
            dimension_semantics=("arbitrary", "arbitrary", "arbitrary"), vmem_limit_bytes=VMEM_LIMIT),
        name="da_attn",
    )(lam_params, g_subln, proj3, proj3, proj3)


def _merge_kernel(x_ref, osb_ref, oda_ref, gsb_ref, gda_ref, wsb_ref, wda_ref, wout_ref, o_ref):
    mixed = (gsb_ref[...].astype(F32) * _dot(osb_ref[...], wsb_ref[...])
             + gda_ref[...].astype(F32) * _dot(oda_ref[...], wda_ref[...]))
    o_ref[...] = x_ref[...] + _dot(mixed.astype(BF16), wout_ref[...])


def _merge(x2, o_sb, o_da, proj, w_sb, w_da, w_out, tm):
    t = x2.shape[0]
    const = lambda i: (0, 0)
    return pl.pallas_call(
        _merge_kernel,
        grid=(t // tm,),
        in_specs=[
            pl.BlockSpec((tm, D_MODEL), lambda i: (i, 0)),
            pl.BlockSpec((tm, SB_WIDTH), lambda i: (i, 0)),
            pl.BlockSpec((tm, DA_V_WIDTH), lambda i: (i, 0)),
            pl.BlockSpec((tm, D_MODEL), lambda i: (i, _GATE)),
            pl.BlockSpec((tm, D_MODEL), lambda i: (i, _GATE + 1)),
            pl.BlockSpec((SB_WIDTH, D_MODEL), const),
            pl.BlockSpec((DA_V_WIDTH, D_MODEL), const),
            pl.BlockSpec((D_MODEL, D_MODEL), const),
        ],
        out_specs=pl.BlockSpec((tm, D_MODEL), lambda i: (i, 0)),
        out_shape=jax.ShapeDtypeStruct((t, D_MODEL), F32),
        compiler_params=pltpu.CompilerParams(
            dimension_semantics=("arbitrary",), vmem_limit_bytes=VMEM_LIMIT),
        name="merge",
    )(x2, o_sb, o_da, proj, proj, w_sb, w_da, w_out)


def _ffn_kernel(x_ref, g_ref, wg_ref, wu_ref, wd_ref, gf_ref, o_ref, *, final_norm):
    x = x_ref[...]
    h = _rms(x, g_ref[...], NORM_EPS).astype(BF16)
    a = (jax.nn.silu(_dot(h, wg_ref[...])) * _dot(h, wu_ref[...])).astype(BF16)
    y = x + _dot(a, wd_ref[...])
    if final_norm:
        y = _rms(y, gf_ref[...], NORM_EPS)
    o_ref[...] = y


def _ffn(x2, g, w_gate, w_up, w_down, g_final, final_norm, tm):
    t = x2.shape[0]
    const = lambda i: (0, 0)
    return pl.pallas_call(
        functools.partial(_ffn_kernel, final_norm=final_norm),
        grid=(t // tm,),
        in_specs=[
            pl.BlockSpec((tm, D_MODEL), lambda i: (i, 0)),
            pl.BlockSpec((1, D_MODEL), const),
            pl.BlockSpec((D_MODEL, D_FF), const),
            pl.BlockSpec((D_MODEL, D_FF), const),
            pl.BlockSpec((D_FF, D_MODEL), const),
            pl.BlockSpec((1, D_MODEL), const),
        ],
        out_specs=pl.BlockSpec((tm, D_MODEL), lambda i: (i, 0)),
        out_shape=jax.ShapeDtypeStruct((t, D_MODEL), F32),
        compiler_params=pltpu.CompilerParams(
            dimension_semantics=("arbitrary",), vmem_limit_bytes=VMEM_LIMIT),
        name="ffn",
    )(x2, g, w_gate, w_up, w_down, g_final)


def _rope_tables(seq_len):
    inv_freq = 1.0 / (ROPE_THETA ** (jnp.arange(0, HEAD_DIM, 2, dtype=F32) / HEAD_DIM))
    ang = jnp.arange(seq_len, dtype=F32)[:, None] * inv_freq[None, :]
    ang = jnp.concatenate([ang, ang], axis=-1)
    sign = jnp.where(jnp.arange(HEAD_DIM) < HEAD_DIM // 2, -1.0, 1.0).astype(F32)
    cos = jnp.tile(jnp.cos(ang), (1, LANES // HEAD_DIM))
    sin = jnp.tile(jnp.sin(ang) * sign[None, :], (1, LANES // HEAD_DIM))
    return cos, sin


def kernel(x, g_mix, w_in, lambda_q1, lambda_k1, lambda_q2, lambda_k2, g_subln, w_branch_sb, w_branch_da,
           w_out, g_ffn, w_ffn_gate, w_ffn_up, w_ffn_down, g_final):
    b, s, d = x.shape
    depth = w_in.shape[0]
    assert d == D_MODEL and s % ATTN_TILE == 0
    tm = min(256, s)
    cos, sin = _rope_tables(s)
    x2 = x.reshape(b * s, d)
    for l in range(depth):
        lambda_init = 0.8 - 0.6 * math.exp(-0.3 * l)
        proj = _in_proj(x2, g_mix[l][None], w_in[l].astype(BF16), cos, sin, s, tm)
        proj3 = proj.reshape(b, s, IN_WIDTH)
        o_sb = _sb_attention(proj3)
        lam_params = jnp.stack([lambda_q1[l], lambda_k1[l], lambda_q2[l], lambda_k2[l]]).astype(F32)
        o_da = _da_attention(proj3, lam_params, g_subln[l][None], lambda_init, ATTN_TILE)
        x2 = _merge(x2, o_sb.reshape(b * s, SB_WIDTH), o_da.reshape(b * s, DA_V_WIDTH), proj,
                    w_branch_sb[l].astype(BF16), w_branch_da[l].astype(BF16), w_out[l].astype(BF16), tm)
        x2 = _ffn(x2, g_ffn[l][None], w_ffn_gate[l].astype(BF16), w_ffn_up[l].astype(BF16),
                  w_ffn_down[l].astype(BF16), g_final[None], l == depth - 1, tm)
    return x2.reshape(b, s, d)
```

```python
import functools
import math

import jax
import jax.numpy as jnp
from jax import lax
from jax.experimental import pallas as pl
from jax.experimental.pallas import tpu as pltpu

D_MODEL = 1024
HEAD_DIM = 64
CHUNK = 64
SB_WIDTH = 512
DA_QK_WIDTH = 512
DA_V_WIDTH = 512
DA_HEADS = 4
N_BRANCH = 2
IN_WIDTH = 3 * SB_WIDTH + 2 * DA_QK_WIDTH + DA_V_WIDTH + N_BRANCH * D_MODEL
D_FF = 2816
ROPE_THETA = 10000.0
NORM_EPS = 1e-6
SUBLN_EPS = 1e-5
QK_SCALE = HEAD_DIM ** -0.5
LOG2E = math.log2(math.e)

LANES = 128
COL_TILE = 512
ATTN_TILE = 256
UNROLL = 4
DA_UNROLL = 8
VMEM_LIMIT = 56 * 1024 * 1024

_QA, _KA, _VA = 0, 4, 8
_QD, _KD, _VD = 12, 16, 20
_GATE = 3

F32 = jnp.float32
BF16 = jnp.bfloat16


def _rms(x, g, eps):
    return x * lax.rsqrt(jnp.mean(x * x, axis=-1, keepdims=True) + eps) * g


def _nt_dot(a, b):
    return lax.dot_general(a, b, (((1,), (1,)), ((), ())), preferred_element_type=F32)


def _dot(a, b):
    return jnp.dot(a, b, preferred_element_type=F32)


def _in_proj_kernel(x_ref, g_ref, w_ref, cos_ref, sin_ref, o_ref):
    h = _rms(x_ref[...], g_ref[...], NORM_EPS).astype(BF16)
    tm = h.shape[0]
    lane = lax.broadcasted_iota(jnp.int32, (tm, LANES), 1)
    first_half = (lane % HEAD_DIM) < (HEAD_DIM // 2)
    cos = cos_ref[...]
    sin = sin_ref[...]

    def rope(a):
        outs = []
        for c in range(a.shape[1] // LANES):
            xs = a[:, c * LANES:(c + 1) * LANES]
            rot = jnp.where(first_half, pltpu.roll(xs, LANES - HEAD_DIM // 2, 1),
                            pltpu.roll(xs, HEAD_DIM // 2, 1))
            outs.append(xs * cos + rot * sin)
        return jnp.concatenate(outs, axis=1)

    for j in range(IN_WIDTH // COL_TILE):
        cols = slice(j * COL_TILE, (j + 1) * COL_TILE)
        a = _dot(h, w_ref[:, cols])
        if j == 0:
            a = a * (QK_SCALE * LOG2E)
        elif j == 3:
            a = rope(a) * (QK_SCALE * LOG2E)
        elif j == 4:
            a = rope(a)
        elif j >= 6:
            a = jax.nn.sigmoid(a)
        o_ref[:, cols] = a.astype(o_ref.dtype)


def _in_proj(x2, g, w, cos, sin, seq, tm):
    t = x2.shape[0]
    pos_blocks = seq // tm
    return pl.pallas_call(
        _in_proj_kernel,
        grid=(t // tm,),
        in_specs=[
            pl.BlockSpec((tm, D_MODEL), lambda i: (i, 0)),
            pl.BlockSpec((1, D_MODEL), lambda i: (0, 0)),
            pl.BlockSpec((D_MODEL, IN_WIDTH), lambda i: (0, 0)),
            pl.BlockSpec((tm, LANES), lambda i: (i % pos_blocks, 0)),
            pl.BlockSpec((tm, LANES), lambda i: (i % pos_blocks, 0)),
        ],
        out_specs=pl.BlockSpec((tm, IN_WIDTH), lambda i: (i, 0)),
        out_shape=jax.ShapeDtypeStruct((t, IN_WIDTH), BF16),
        compiler_params=pltpu.CompilerParams(
            dimension_semantics=("arbitrary",), vmem_limit_bytes=VMEM_LIMIT),
        name="in_proj",
    )(x2, g, w, cos, sin)


def _tile_schedule(n_tiles):
    steps = [(i, i) for i in range(n_tiles)]
    steps += [(i, j) for i in range(n_tiles) for j in range(i - 1, -1, -1)]
    return steps


def _sb_kernel(qi_tab, kj_tab, q_ref, k_ref, v_ref, o_ref, qs, vm, u2, racc, oacc,
               z0, z1, hl0, hl1, rs0, rs1, w0, w1, sc0, sc1, *, n_steps, n_diag):
    t = ATTN_TILE
    s_len = q_ref.shape[1]
    lane = lax.broadcasted_iota(jnp.int32, (t, LANES), 1)
    head0 = lane < HEAD_DIM

    def prep(c, carry):
        rows = pl.ds(pl.multiple_of(c * t, t), t)
        qc = q_ref[0, rows, :]
        vc = v_ref[0, rows, :]
        zero = jnp.zeros_like(qc)
        qs[0, rows, :] = jnp.where(head0, qc, zero)
        qs[1, rows, :] = jnp.where(head0, zero, qc)
        vm[0, rows, :] = jnp.where(head0, vc, zero)
        vm[1, rows, :] = jnp.where(head0, zero, vc)
        oacc[rows, :] = jnp.zeros((t, LANES), F32)
        racc[0, rows, :] = jnp.zeros((t, LANES), F32)
        racc[1, rows, :] = jnp.zeros((t, LANES), F32)
        return carry

    lax.fori_loop(0, s_len // t, prep, 0)
    row = lax.broadcasted_iota(jnp.int32, (t, t), 0)
    col = lax.broadcasted_iota(jnp.int32, (t, t), 1)
    suffix = jnp.where(row >= col, 1.0, 0.0).astype(BF16)
    u2[0:t, :] = suffix
    u2[t:2 * t, :] = suffix

    def rows_of(tab, s):
        return pl.ds(pl.multiple_of(tab[s] * t, t), t)

    def stage_a(s, z_s, hl, rs_s, masked):
        qrows = rows_of(qi_tab, s)
        q2 = jnp.concatenate([qs[0, qrows, :], qs[1, qrows, :]], axis=0)
        z = _nt_dot(q2, k_ref[0, rows_of(kj_tab, s), :])
        neg_abs = lax.bitcast_convert_type(
            lax.bitcast_convert_type(z, jnp.uint32) | jnp.uint32(0x80000000), F32)
        sp = jnp.maximum(z, 0.0) + jnp.log2(1.0 + jnp.exp2(neg_abs))
        if masked:
            r2 = lax.broadcasted_iota(jnp.int32, (2 * t, t), 0)
            c2 = lax.broadcasted_iota(jnp.int32, (2 * t, t), 1)
            earlier = c2 < (r2 % t)
            sp = jnp.where(earlier, sp, 0.0)
            z = jnp.where(earlier, z, -jnp.inf)
        z_s[...] = z
        hi = sp.astype(BF16)
        hl[:, 0:t] = hi
        hl[:, t:2 * t] = (sp - hi.astype(F32)).astype(BF16)
        rs_s[...] = jnp.broadcast_to(jnp.sum(sp, axis=1, keepdims=True), (2 * t, LANES))

    def stage_b(s, z_s, hl, rs_s, w, sc):
        wv = jnp.exp2(z_s[...] - _dot(hl[...], u2[...])).astype(BF16)
        w[:, 0:t] = wv[:t]
        w[:, t:2 * t] = wv[t:]
        qrows = rows_of(qi_tab, s)
        rs = rs_s[...]
        r0 = racc[0, qrows, :]
        r1 = racc[1, qrows, :]
        racc[0, qrows, :] = r0 + rs[:t]
        racc[1, qrows, :] = r1 + rs[t:]
        sc[...] = jnp.exp2(-jnp.where(head0, r0, r1))

    def stage_c(s, w, sc):
        krows = rows_of(kj_tab, s)
        v2 = jnp.concatenate([vm[0, krows, :], vm[1, krows, :]], axis=0)
        oacc[rows_of(qi_tab, s), :] += _dot(w[...], v2) * sc[...]

    slot = ((z0, hl0, rs0, w0, sc0), (z1, hl1, rs1, w1, sc1))

    def step(s, parity, masked):
        z_a, hl_a, rs_a, w_c, sc_c = slot[parity]
        z_b, hl_b, rs_b, w_b, sc_b = slot[1 - parity]
        stage_a(s, z_a, hl_a, rs_a, masked)
        stage_b(s - 1, z_b, hl_b, rs_b, w_b, sc_b)
        stage_c(s - 2, w_c, sc_c)

    def steps_per_trip(n, masked):
        def body(i, carry):
            for u in range(n):
                step(n * i + u, u % 2, masked)
            return carry
        return body

    stage_a(0, z0, hl0, rs0, True)
    stage_a(1, z1, hl1, rs1, True)
    stage_b(0, z0, hl0, rs0, w0, sc0)
    lax.fori_loop(1, n_diag // 2, steps_per_trip(2, True), 0)
    lax.fori_loop(n_diag // UNROLL, n_steps // UNROLL, steps_per_trip(UNROLL, False), 0)
    stage_b(n_steps - 1, z1, hl1, rs1, w1, sc1)
    stage_c(n_steps - 2, w0, sc0)
    stage_c(n_steps - 1, w1, sc1)
    o_ref[0] = oacc[...].astype(o_ref.dtype)


def _sb_attention(proj3):
    b, s, _ = proj3.shape
    t = ATTN_TILE
    n_tiles = s // t
    steps = _tile_schedule(n_tiles)
    assert n_tiles % UNROLL == 0 and len(steps) % UNROLL == 0
    qi_tab = jnp.asarray([q for q, _ in steps], jnp.int32)
    kj_tab = jnp.asarray([k for _, k in steps], jnp.int32)
    col_block = lambda off: pl.BlockSpec((1, s, LANES), lambda bi, p, *_: (bi, 0, off + p))
    return pl.pallas_call(
        functools.partial(_sb_kernel, n_steps=len(steps), n_diag=n_tiles),
        grid_spec=pltpu.PrefetchScalarGridSpec(
            num_scalar_prefetch=2,
            grid=(b, SB_WIDTH // LANES),
            in_specs=[col_block(_QA), col_block(_KA), col_block(_VA)],
            out_specs=pl.BlockSpec((1, s, LANES), lambda bi, p, *_: (bi, 0, p)),
            scratch_shapes=[
                pltpu.VMEM((2, s, LANES), BF16),
                pltpu.VMEM((2, s, LANES), BF16),
                pltpu.VMEM((2 * t, t), BF16),
                pltpu.VMEM((2, s, LANES), F32),
                pltpu.VMEM((s, LANES), F32),
                pltpu.VMEM((2 * t, t), F32), pltpu.VMEM((2 * t, t), F32),
                pltpu.VMEM((2 * t, 2 * t), BF16), pltpu.VMEM((2 * t, 2 * t), BF16),
                pltpu.VMEM((2 * t, LANES), F32), pltpu.VMEM((2 * t, LANES), F32),
                pltpu.VMEM((t, 2 * t), BF16), pltpu.VMEM((t, 2 * t), BF16),
                pltpu.VMEM((t, LANES), F32), pltpu.VMEM((t, LANES), F32),
            ]),
        out_shape=jax.ShapeDtypeStruct((b, s, SB_WIDTH), BF16),
        compiler_params=pltpu.CompilerParams(
            dimension_semantics=("arbitrary", "arbitrary"), vmem_limit_bytes=VMEM_LIMIT),
        name="sb_attn",
    )(qi_tab, kj_tab, proj3, proj3, proj3)


def _da_kernel(qi_tab, kj_tab, lam_ref, g_ref, q_ref, k_ref, v_ref, o_ref, qs, acc, macc, lacc,
               p0, p1, al0, al1, *, lambda_init, n_steps, n_diag):
    t = ATTN_TILE
    s_len = q_ref.shape[1]
    lane = lax.broadcasted_iota(jnp.int32, (t, LANES), 1)
    map0 = lane < HEAD_DIM

    def prep(c, carry):
        rows = pl.ds(pl.multiple_of(c * t, t), t)
        qc = q_ref[0, rows, :]
        zero = jnp.zeros_like(qc)
        qs[0, rows, :] = jnp.where(map0, qc, zero)
        qs[1, rows, :] = jnp.where(map0, zero, qc)
        for c2 in range(2):
            acc[c2, rows, :] = jnp.zeros((t, LANES), F32)
            lacc[c2, rows, :] = jnp.zeros((t, LANES), F32)
            macc[c2, rows, :] = jnp.full((t, LANES), -jnp.inf, F32)
        return carry

    lax.fori_loop(0, s_len // t, prep, 0)

    def rows_of(tab, s):
        return pl.ds(pl.multiple_of(tab[s] * t, t), t)

    def stage_a(s, p_s, al_s, masked):
        qrows = rows_of(qi_tab, s)
        q2 = jnp.concatenate([qs[0, qrows, :], qs[1, qrows, :]], axis=0)
        sc = _nt_dot(q2, k_ref[0, rows_of(kj_tab, s), :])
        if masked:
            r2 = lax.broadcasted_iota(jnp.int32, (2 * t, t), 0)
            c2 = lax.broadcasted_iota(jnp.int32, (2 * t, t), 1)
            sc = jnp.where((c2 // CHUNK) <= ((r2 % t) // CHUNK), sc, -jnp.inf)
        m_prev = jnp.concatenate([macc[0, qrows, :], macc[1, qrows, :]], axis=0)
        m_new = jnp.maximum(m_prev, jnp.max(sc, axis=1, keepdims=True))
        p = jnp.exp2(sc - jnp.concatenate([m_new] * (t // LANES), axis=1))
        alpha = jnp.exp2(m_prev - m_new)
        l_prev = jnp.concatenate([lacc[0, qrows, :], lacc[1, qrows, :]], axis=0)
        l_new = alpha * l_prev + jnp.sum(p, axis=1, keepdims=True)
        for c in range(2):
            macc[c, qrows, :] = m_new[c * t:(c + 1) * t]
            lacc[c, qrows, :] = l_new[c * t:(c + 1) * t]
        p_s[...] = p.astype(BF16)
        al_s[...] = alpha

    def stage_b(s, p_s, al_s):
        qrows = rows_of(qi_tab, s)
        pv = _dot(p_s[...], v_ref[0, rows_of(kj_tab, s), :])
        al = al_s[...]
        for c in range(2):
            acc[c, qrows, :] = al[c * t:(c + 1) * t] * acc[c, qrows, :] + pv[c * t:(c + 1) * t]

    slot = ((p0, al0), (p1, al1))

    def step(s, parity, masked):
        stage_a(s, *slot[parity], masked)
        stage_b(s - 1, *slot[1 - parity])

    def steps_per_trip(masked):
        def body(i, carry):
            for u in range(DA_UNROLL):
                step(DA_UNROLL * i + u, u % 2, masked)
            return carry
        return body

    stage_a(0, p0, al0, True)
    for u in range(1, DA_UNROLL):
        step(u, u % 2, True)
    lax.fori_loop(1, n_diag // DA_UNROLL, steps_per_trip(True), 0)
    lax.fori_loop(n_diag // DA_UNROLL, n_steps // DA_UNROLL, steps_per_trip(False), 0)
    stage_b(n_steps - 1, *slot[(n_steps - 1) % 2])

    lam_p = lam_ref[...]
    lam = (jnp.exp(jnp.sum(lam_p[0:1] * lam_p[1:2], axis=1, keepdims=True))
           - jnp.exp(jnp.sum(lam_p[2:3] * lam_p[3:4], axis=1, keepdims=True)) + lambda_init)
    g = g_ref[...]

    def finish(c, carry):
        rows = pl.ds(pl.multiple_of(c * t, t), t)
        o = acc[0, rows, :] / lacc[0, rows, :] - lam * (acc[1, rows, :] / lacc[1, rows, :])
        o_ref[0, rows, :] = (_rms(o, g, SUBLN_EPS) * (1.0 - lambda_init)).astype(o_ref.dtype)
        return carry

    lax.fori_loop(0, s_len // t, finish, 0)


def _da_attention(proj3, lam_params, g_subln, lambda_init):
    b, s, _ = proj3.shape
    t = ATTN_TILE
    n_tiles = s // t
    steps = _tile_schedule(n_tiles)
    assert n_tiles % DA_UNROLL == 0 and len(steps) % DA_UNROLL == 0
    qi_tab = jnp.asarray([q for q, _ in steps], jnp.int32)
    kj_tab = jnp.asarray([k for _, k in steps], jnp.int32)
    col_block = lambda off: pl.BlockSpec((1, s, LANES), lambda bi, h, *_: (bi, 0, off + h))
    return pl.pallas_call(
        functools.partial(_da_kernel, lambda_init=lambda_init, n_steps=len(steps), n_diag=n_tiles),
        grid_spec=pltpu.PrefetchScalarGridSpec(
            num_scalar_prefetch=2,
            grid=(b, DA_HEADS),
            in_specs=[
                pl.BlockSpec((4, HEAD_DIM), lambda bi, h, *_: (0, 0)),
                pl.BlockSpec((1, LANES), lambda bi, h, *_: (0, 0)),
                col_block(_QD), col_block(_KD), col_block(_VD),
            ],
            out_specs=pl.BlockSpec((1, s, LANES), lambda bi, h, *_: (bi, 0, h)),
            scratch_shapes=[
                pltpu.VMEM((2, s, LANES), BF16),
                pltpu.VMEM((2, s, LANES), F32),
                pltpu.VMEM((2, s, LANES), F32),
                pltpu.VMEM((2, s, LANES), F32),
                pltpu.VMEM((2 * t, t), BF16), pltpu.VMEM((2 * t, t), BF16),
                pltpu.VMEM((2 * t, LANES), F32), pltpu.VMEM((2 * t, LANES), F32),
            ]),
        out_shape=jax.ShapeDtypeStruct((b, s, DA_V_WIDTH), BF16),
        compiler_params=pltpu.CompilerParams(
            dimension_semantics=("arbitrary", "arbitrary"), vmem_limit_bytes=VMEM_LIMIT),
        name="da_attn",
    )(qi_tab, kj_tab, lam_params, g_subln, proj3, proj3, proj3)


def _merge_kernel(x_ref, osb_ref, oda_ref, gsb_ref, gda_ref, wsb_ref, wda_ref, wout_ref, o_ref):
    mixed = (gsb_ref[...].astype(F32) * _dot(osb_ref[...], wsb_ref[...])
             + gda_ref[...].astype(F32) * _dot(oda_ref[...], wda_ref[...]))
    o_ref[...] = x_ref[...] + _dot(mixed.astype(BF16), wout_ref[...])


def _merge(x2, o_sb, o_da, proj, w_sb, w_da, w_out, tm):
    t = x2.shape[0]
    const = lambda i: (0, 0)
    return pl.pallas_call(
        _merge_kernel,
        grid=(t // tm,),
        in_specs=[
            pl.BlockSpec((tm, D_MODEL), lambda i: (i, 0)),
            pl.BlockSpec((tm, SB_WIDTH), lambda i: (i, 0)),
            pl.BlockSpec((tm, DA_V_WIDTH), lambda i: (i, 0)),
            pl.BlockSpec((tm, D_MODEL), lambda i: (i, _GATE)),
            pl.BlockSpec((tm, D_MODEL), lambda i: (i, _GATE + 1)),
            pl.BlockSpec((SB_WIDTH, D_MODEL), const),
            pl.BlockSpec((DA_V_WIDTH, D_MODEL), const),
            pl.BlockSpec((D_MODEL, D_MODEL), const),
        ],
        out_specs=pl.BlockSpec((tm, D_MODEL), lambda i: (i, 0)),
        out_shape=jax.ShapeDtypeStruct((t, D_MODEL), F32),
        compiler_params=pltpu.CompilerParams(
            dimension_semantics=("arbitrary",), vmem_limit_bytes=VMEM_LIMIT),
        name="merge",
    )(x2, o_sb, o_da, proj, proj, w_sb, w_da, w_out)


def _ffn_kernel(x_ref, g_ref, wg_ref, wu_ref, wd_ref, gf_ref, o_ref, *, final_norm):
    x = x_ref[...]
    h = _rms(x, g_ref[...], NORM_EPS).astype(BF16)
    a = (jax.nn.silu(_dot(h, wg_ref[...])) * _dot(h, wu_ref[...])).astype(BF16)
    y = x + _dot(a, wd_ref[...])
    if final_norm:
        y = _rms(y, gf_ref[...], NORM_EPS)
    o_ref[...] = y


def _ffn(x2, g, w_gate, w_up, w_down, g_final, final_norm, tm):
    t = x2.shape[0]
    const = lambda i: (0, 0)
    return pl.pallas_call(
        functools.partial(_ffn_kernel, final_norm=final_norm),
        grid=(t // tm,),
        in_specs=[
            pl.BlockSpec((tm, D_MODEL), lambda i: (i, 0)),
            pl.BlockSpec((1, D_MODEL), const),
            pl.BlockSpec((D_MODEL, D_FF), const),
            pl.BlockSpec((D_MODEL, D_FF), const),
            pl.BlockSpec((D_FF, D_MODEL), const),
            pl.BlockSpec((1, D_MODEL), const),
        ],
        out_specs=pl.BlockSpec((tm, D_MODEL), lambda i: (i, 0)),
        out_shape=jax.ShapeDtypeStruct((t, D_MODEL), F32),
        compiler_params=pltpu.CompilerParams(
            dimension_semantics=("arbitrary",), vmem_limit_bytes=VMEM_LIMIT),
        name="ffn",
    )(x2, g, w_gate, w_up, w_down, g_final)


def _rope_tables(seq_len):
    inv_freq = 1.0 / (ROPE_THETA ** (jnp.arange(0, HEAD_DIM, 2, dtype=F32) / HEAD_DIM))
    ang = jnp.arange(seq_len, dtype=F32)[:, None] * inv_freq[None, :]
    ang = jnp.concatenate([ang, ang], axis=-1)
    sign = jnp.where(jnp.arange(HEAD_DIM) < HEAD_DIM // 2, -1.0, 1.0).astype(F32)
    cos = jnp.tile(jnp.cos(ang), (1, LANES // HEAD_DIM))
    sin = jnp.tile(jnp.sin(ang) * sign[None, :], (1, LANES // HEAD_DIM))
    return cos, sin


def kernel(x, g_mix, w_in, lambda_q1, lambda_k1, lambda_q2, lambda_k2, g_subln, w_branch_sb, w_branch_da,
           w_out, g_ffn, w_ffn_gate, w_ffn_up, w_ffn_down, g_final):
    b, s, d = x.shape
    depth = w_in.shape[0]
    assert d == D_MODEL and s % ATTN_TILE == 0
    tm = min(256, s)
    cos, sin = _rope_tables(s)
    x2 = x.reshape(b * s, d)
    for l in range(depth):
        lambda_init = 0.8 - 0.6 * math.exp(-0.3 * l)
        proj = _in_proj(x2, g_mix[l][None], w_in[l].astype(BF16), cos, sin, s, tm)
        proj3 = proj.reshape(b, s, IN_WIDTH)
        o_sb = _sb_attention(proj3)
        lam_params = jnp.stack([lambda_q1[l], lambda_k1[l], lambda_q2[l], lambda_k2[l]]).astype(F32)
        o_da = _da_attention(proj3, lam_params, g_subln[l][None], lambda_init)
        x2 = _merge(x2, o_sb.reshape(b * s, SB_WIDTH), o_da.reshape(b * s, DA_V_WIDTH), proj,
                    w_branch_sb[l].astype(BF16), w_branch_da[l].astype(BF16), w_out[l].astype(BF16), tm)
        x2 = _ffn(x2, g_ffn[l][None], w_ffn_gate[l].astype(BF16), w_ffn_up[l].astype(BF16),
                  w_ffn_down[l].astype(BF16), g_final[None], l == depth - 1, tm)
    return x2.reshape(b, s, d)
```

```python
import functools
import math

import jax
import jax.numpy as jnp
from jax import lax
from jax.experimental import pallas as pl
from jax.experimental.pallas import tpu as pltpu

D_MODEL = 1024
HEAD_DIM = 64
CHUNK = 64
SB_WIDTH = 512
DA_QK_WIDTH = 512
DA_V_WIDTH = 512
DA_HEADS = 4
N_BRANCH = 2
IN_WIDTH = 3 * SB_WIDTH + 2 * DA_QK_WIDTH + DA_V_WIDTH + N_BRANCH * D_MODEL
D_FF = 2816
ROPE_THETA = 10000.0
NORM_EPS = 1e-6
SUBLN_EPS = 1e-5
QK_SCALE = HEAD_DIM ** -0.5
LOG2E = math.log2(math.e)

LANES = 128
COL_TILE = 512
ATTN_TILE = 256
SB_PHASE = 4
DA_UNROLL = 8
VMEM_LIMIT = 56 * 1024 * 1024

_QA, _KA, _VA = 0, 4, 8
_QD, _KD, _VD = 12, 16, 20
_GATE = 3

F32 = jnp.float32
BF16 = jnp.bfloat16


def _rms(x, g, eps):
    return x * lax.rsqrt(jnp.mean(x * x, axis=-1, keepdims=True) + eps) * g


def _nt_dot(a, b):
    return lax.dot_general(a, b, (((1,), (1,)), ((), ())), preferred_element_type=F32)


def _dot(a, b):
    return jnp.dot(a, b, preferred_element_type=F32)


def _in_proj_kernel(x_ref, g_ref, w_ref, cos_ref, sin_ref, o_ref):
    h = _rms(x_ref[...], g_ref[...], NORM_EPS).astype(BF16)
    tm = h.shape[0]
    lane = lax.broadcasted_iota(jnp.int32, (tm, LANES), 1)
    first_half = (lane % HEAD_DIM) < (HEAD_DIM // 2)
    cos = cos_ref[...]
    sin = sin_ref[...]

    def rope(a):
        outs = []
        for c in range(a.shape[1] // LANES):
            xs = a[:, c * LANES:(c + 1) * LANES]
            rot = jnp.where(first_half, pltpu.roll(xs, LANES - HEAD_DIM // 2, 1),
                            pltpu.roll(xs, HEAD_DIM // 2, 1))
            outs.append(xs * cos + rot * sin)
        return jnp.concatenate(outs, axis=1)

    for j in range(IN_WIDTH // COL_TILE):
        cols = slice(j * COL_TILE, (j + 1) * COL_TILE)
        a = _dot(h, w_ref[:, cols])
        if j == 0:
            a = a * (QK_SCALE * LOG2E)
        elif j == 3:
            a = rope(a) * (QK_SCALE * LOG2E)
        elif j == 4:
            a = rope(a)
        elif j >= 6:
            a = jax.nn.sigmoid(a)
        o_ref[:, cols] = a.astype(o_ref.dtype)


def _in_proj(x2, g, w, cos, sin, seq, tm):
    t = x2.shape[0]
    pos_blocks = seq // tm
    return pl.pallas_call(
        _in_proj_kernel,
        grid=(t // tm,),
        in_specs=[
            pl.BlockSpec((tm, D_MODEL), lambda i: (i, 0)),
            pl.BlockSpec((1, D_MODEL), lambda i: (0, 0)),
            pl.BlockSpec((D_MODEL, IN_WIDTH), lambda i: (0, 0)),
            pl.BlockSpec((tm, LANES), lambda i: (i % pos_blocks, 0)),
            pl.BlockSpec((tm, LANES), lambda i: (i % pos_blocks, 0)),
        ],
        out_specs=pl.BlockSpec((tm, IN_WIDTH), lambda i: (i, 0)),
        out_shape=jax.ShapeDtypeStruct((t, IN_WIDTH), BF16),
        compiler_params=pltpu.CompilerParams(
            dimension_semantics=("arbitrary",), vmem_limit_bytes=VMEM_LIMIT),
        name="in_proj",
    )(x2, g, w, cos, sin)


def _tile_schedule(n_tiles):
    steps = [(i, i) for i in range(n_tiles)]
    steps += [(i, j) for i in range(n_tiles) for j in range(i - 1, -1, -1)]
    return steps


def _sb_kernel(qi_tab, kj_tab, q_ref, k_ref, v_ref, o_ref, qs, vm, u2, racc, oacc,
               z_sl, hl_sl, rs_sl, w_sl, sc_sl, *, n_steps, n_diag):
    t = ATTN_TILE
    s_len = q_ref.shape[1]
    lane = lax.broadcasted_iota(jnp.int32, (t, LANES), 1)
    head0 = lane < HEAD_DIM

    def prep(c, carry):
        rows = pl.ds(pl.multiple_of(c * t, t), t)
        qc = q_ref[0, rows, :]
        vc = v_ref[0, rows, :]
        zero = jnp.zeros_like(qc)
        qs[0, rows, :] = jnp.where(head0, qc, zero)
        qs[1, rows, :] = jnp.where(head0, zero, qc)
        vm[0, rows, :] = jnp.where(head0, vc, zero)
        vm[1, rows, :] = jnp.where(head0, zero, vc)
        oacc[rows, :] = jnp.zeros((t, LANES), F32)
        racc[0, rows, :] = jnp.zeros((t, LANES), F32)
        racc[1, rows, :] = jnp.zeros((t, LANES), F32)
        return carry

    lax.fori_loop(0, s_len // t, prep, 0)
    row = lax.broadcasted_iota(jnp.int32, (t, t), 0)
    col = lax.broadcasted_iota(jnp.int32, (t, t), 1)
    u2[...] = jnp.where(row > col, 1.0, 0.0).astype(BF16)

    def rows_of(tab, s):
        return pl.ds(pl.multiple_of(tab[s] * t, t), t)

    def stage_a(s, z_s, hl, rs_s, masked):
        qrows = rows_of(qi_tab, s)
        q2 = jnp.concatenate([qs[0, qrows, :], qs[1, qrows, :]], axis=0)
        z = _nt_dot(q2, k_ref[0, rows_of(kj_tab, s), :])
        neg_abs = lax.bitcast_convert_type(
            lax.bitcast_convert_type(z, jnp.uint32) | jnp.uint32(0x80000000), F32)
        sp = jnp.maximum(z, 0.0) + jnp.log2(1.0 + jnp.exp2(neg_abs))
        own = z - sp
        if masked:
            r2 = lax.broadcasted_iota(jnp.int32, (2 * t, t), 0)
            c2 = lax.broadcasted_iota(jnp.int32, (2 * t, t), 1)
            earlier = c2 < (r2 % t)
            sp = jnp.where(earlier, sp, 0.0)
            own = jnp.where(earlier, own, -jnp.inf)
        z_s[...] = own
        hl[...] = sp.astype(BF16)
        rs_s[...] = jnp.broadcast_to(jnp.sum(sp, axis=1, keepdims=True), (2 * t, LANES))

    def stage_b(s, z_s, hl, rs_s, w, sc):
        wv = jnp.exp2(z_s[...] - _dot(hl[...], u2[...])).astype(BF16)
        w[:, 0:t] = wv[:t]
        w[:, t:2 * t] = wv[t:]
        qrows = rows_of(qi_tab, s)
        rs = rs_s[...]
        r0 = racc[0, qrows, :]
        r1 = racc[1, qrows, :]
        racc[0, qrows, :] = r0 + rs[:t]
        racc[1, qrows, :] = r1 + rs[t:]
        sc[...] = jnp.exp2(-jnp.where(head0, r0, r1))

    def stage_c(s, w, sc):
        krows = rows_of(kj_tab, s)
        v2 = jnp.concatenate([vm[0, krows, :], vm[1, krows, :]], axis=0)
        oacc[rows_of(qi_tab, s), :] += _dot(w[...], v2) * sc[...]

    ph = SB_PHASE

    def phase(base, a_set, masked, do_a=True, do_b=True, do_c=True):
        b_set = 1 - a_set
        for u in range(ph):
            if do_a:
                stage_a(base + u, z_sl.at[a_set, u], hl_sl.at[a_set, u], rs_sl.at[a_set, u], masked)
            if do_b:
                stage_b(base - ph + u, z_sl.at[b_set, u], hl_sl.at[b_set, u], rs_sl.at[b_set, u],
                        w_sl.at[b_set, u], sc_sl.at[b_set, u])
            if do_c:
                stage_c(base - 2 * ph + u, w_sl.at[a_set, u], sc_sl.at[a_set, u])

    def two_phases(masked):
        def body(j, carry):
            phase(2 * ph * j, 0, masked)
            phase(2 * ph * j + ph, 1, masked)
            return carry
        return body

    phase(0, 0, True, do_b=False, do_c=False)
    phase(ph, 1, True, do_c=False)
    lax.fori_loop(1, n_diag // (2 * ph), two_phases(True), 0)
    lax.fori_loop(n_diag // (2 * ph), n_steps // (2 * ph), two_phases(False), 0)
    phase(n_steps, 0, False, do_a=False)
    phase(n_steps + ph, 1, False, do_a=False, do_b=False)
    o_ref[0] = oacc[...].astype(o_ref.dtype)


def _sb_attention(proj3):
    b, s, _ = proj3.shape
    t = ATTN_TILE
    n_tiles = s // t
    steps = _tile_schedule(n_tiles)
    assert n_tiles % (2 * SB_PHASE) == 0 and len(steps) % (2 * SB_PHASE) == 0
    qi_tab = jnp.asarray([q for q, _ in steps], jnp.int32)
    kj_tab = jnp.asarray([k for _, k in steps], jnp.int32)
    col_block = lambda off: pl.BlockSpec((1, s, LANES), lambda bi, p, *_: (bi, 0, off + p))
    return pl.pallas_call(
        functools.partial(_sb_kernel, n_steps=len(steps), n_diag=n_tiles),
        grid_spec=pltpu.PrefetchScalarGridSpec(
            num_scalar_prefetch=2,
            grid=(b, SB_WIDTH // LANES),
            in_specs=[col_block(_QA), col_block(_KA), col_block(_VA)],
            out_specs=pl.BlockSpec((1, s, LANES), lambda bi, p, *_: (bi, 0, p)),
            scratch_shapes=[
                pltpu.VMEM((2, s, LANES), BF16),
                pltpu.VMEM((2, s, LANES), BF16),
                pltpu.VMEM((t, t), BF16),
                pltpu.VMEM((2, s, LANES), F32),
                pltpu.VMEM((s, LANES), F32),
                pltpu.VMEM((2, SB_PHASE, 2 * t, t), F32),
                pltpu.VMEM((2, SB_PHASE, 2 * t, t), BF16),
                pltpu.VMEM((2, SB_PHASE, 2 * t, LANES), F32),
                pltpu.VMEM((2, SB_PHASE, t, 2 * t), BF16),
                pltpu.VMEM((2, SB_PHASE, t, LANES), F32),
            ]),
        out_shape=jax.ShapeDtypeStruct((b, s, SB_WIDTH), BF16),
        compiler_params=pltpu.CompilerParams(
            dimension_semantics=("arbitrary", "arbitrary"), vmem_limit_bytes=VMEM_LIMIT),
        name="sb_attn",
    )(qi_tab, kj_tab, proj3, proj3, proj3)


def _da_kernel(qi_tab, kj_tab, lam_ref, g_ref, q_ref, k_ref, v_ref, o_ref, qs, acc, macc, lacc,
               p0, p1, al0, al1, *, lambda_init, n_steps, n_diag):
    t = ATTN_TILE
    s_len = q_ref.shape[1]
    lane = lax.broadcasted_iota(jnp.int32, (t, LANES), 1)
    map0 = lane < HEAD_DIM

    def prep(c, carry):
        rows = pl.ds(pl.multiple_of(c * t, t), t)
        qc = q_ref[0, rows, :]
        zero = jnp.zeros_like(qc)
        qs[0, rows, :] = jnp.where(map0, qc, zero)
        qs[1, rows, :] = jnp.where(map0, zero, qc)
        for c2 in range(2):
            acc[c2, rows, :] = jnp.zeros((t, LANES), F32)
            lacc[c2, rows, :] = jnp.zeros((t, LANES), F32)
            macc[c2, rows, :] = jnp.full((t, LANES), -jnp.inf, F32)
        return carry

    lax.fori_loop(0, s_len // t, prep, 0)

    def rows_of(tab, s):
        return pl.ds(pl.multiple_of(tab[s] * t, t), t)

    def stage_a(s, p_s, al_s, masked):
        qrows = rows_of(qi_tab, s)
        q2 = jnp.concatenate([qs[0, qrows, :], qs[1, qrows, :]], axis=0)
        sc = _nt_dot(q2, k_ref[0, rows_of(kj_tab, s), :])
        if masked:
            r2 = lax.broadcasted_iota(jnp.int32, (2 * t, t), 0)
            c2 = lax.broadcasted_iota(jnp.int32, (2 * t, t), 1)
            sc = jnp.where((c2 // CHUNK) <= ((r2 % t) // CHUNK), sc, -jnp.inf)
        m_prev = jnp.concatenate([macc[0, qrows, :], macc[1, qrows, :]], axis=0)
        m_new = jnp.maximum(m_prev, jnp.max(sc, axis=1, keepdims=True))
        p = jnp.exp2(sc - jnp.concatenate([m_new] * (t // LANES), axis=1))
        alpha = jnp.exp2(m_prev - m_new)
        l_prev = jnp.concatenate([lacc[0, qrows, :], lacc[1, qrows, :]], axis=0)
        l_new = alpha * l_prev + jnp.sum(p, axis=1, keepdims=True)
        for c in range(2):
            macc[c, qrows, :] = m_new[c * t:(c + 1) * t]
            lacc[c, qrows, :] = l_new[c * t:(c + 1) * t]
        p_s[...] = p.astype(BF16)
        al_s[...] = alpha

    def stage_b(s, p_s, al_s):
        qrows = rows_of(qi_tab, s)
        pv = _dot(p_s[...], v_ref[0, rows_of(kj_tab, s), :])
        al = al_s[...]
        for c in range(2):
            acc[c, qrows, :] = al[c * t:(c + 1) * t] * acc[c, qrows, :] + pv[c * t:(c + 1) * t]

    slot = ((p0, al0), (p1, al1))

    def step(s, parity, masked):
        stage_a(s, *slot[parity], masked)
        stage_b(s - 1, *slot[1 - parity])

    def steps_per_trip(masked):
        def body(i, carry):
            for u in range(DA_UNROLL):
                step(DA_UNROLL * i + u, u % 2, masked)
            return carry
        return body

    stage_a(0, p0, al0, True)
    for u in range(1, DA_UNROLL):
        step(u, u % 2, True)
    lax.fori_loop(1, n_diag // DA_UNROLL, steps_per_trip(True), 0)
    lax.fori_loop(n_diag // DA_UNROLL, n_steps // DA_UNROLL, steps_per_trip(False), 0)
    stage_b(n_steps - 1, *slot[(n_steps - 1) % 2])

    lam_p = lam_ref[...]
    lam = (jnp.exp(jnp.sum(lam_p[0:1] * lam_p[1:2], axis=1, keepdims=True))
           - jnp.exp(jnp.sum(lam_p[2:3] * lam_p[3:4], axis=1, keepdims=True)) + lambda_init)
    g = g_ref[...]

    def finish(c, carry):
        rows = pl.ds(pl.multiple_of(c * t, t), t)
        o = acc[0, rows, :] / lacc[0, rows, :] - lam * (acc[1, rows, :] / lacc[1, rows, :])
        o_ref[0, rows, :] = (_rms(o, g, SUBLN_EPS) * (1.0 - lambda_init)).astype(o_ref.dtype)
        return carry

    lax.fori_loop(0, s_len // t, finish, 0)


def _da_attention(proj3, lam_params, g_subln, lambda_init):
    b, s, _ = proj3.shape
    t = ATTN_TILE
    n_tiles = s // t
    steps = _tile_schedule(n_tiles)
    assert n_tiles % DA_UNROLL == 0 and len(steps) % DA_UNROLL == 0
    qi_tab = jnp.asarray([q for q, _ in steps], jnp.int32)
    kj_tab = jnp.asarray([k for _, k in steps], jnp.int32)
    col_block = lambda off: pl.BlockSpec((1, s, LANES), lambda bi, h, *_: (bi, 0, off + h))
    return pl.pallas_call(
        functools.partial(_da_kernel, lambda_init=lambda_init, n_steps=len(steps), n_diag=n_tiles),
        grid_spec=pltpu.PrefetchScalarGridSpec(
            num_scalar_prefetch=2,
            grid=(b, DA_HEADS),
            in_specs=[
                pl.BlockSpec((4, HEAD_DIM), lambda bi, h, *_: (0, 0)),
                pl.BlockSpec((1, LANES), lambda bi, h, *_: (0, 0)),
                col_block(_QD), col_block(_KD), col_block(_VD),
            ],
            out_specs=pl.BlockSpec((1, s, LANES), lambda bi, h, *_: (bi, 0, h)),
            scratch_shapes=[
                pltpu.VMEM((2, s, LANES), BF16),
                pltpu.VMEM((2, s, LANES), F32),
                pltpu.VMEM((2, s, LANES), F32),
                pltpu.VMEM((2, s, LANES), F32),
                pltpu.VMEM((2 * t, t), BF16), pltpu.VMEM((2 * t, t), BF16),
                pltpu.VMEM((2 * t, LANES), F32), pltpu.VMEM((2 * t, LANES), F32),
            ]),
        out_shape=jax.ShapeDtypeStruct((b, s, DA_V_WIDTH), BF16),
        compiler_params=pltpu.CompilerParams(
            dimension_semantics=("arbitrary", "arbitrary"), vmem_limit_bytes=VMEM_LIMIT),
        name="da_attn",
    )(qi_tab, kj_tab, lam_params, g_subln, proj3, proj3, proj3)


def _merge_kernel(x_ref, osb_ref, oda_ref, gsb_ref, gda_ref, wsb_ref, wda_ref, wout_ref, o_ref):
    mixed = (gsb_ref[...].astype(F32) * _dot(osb_ref[...], wsb_ref[...])
             + gda_ref[...].astype(F32) * _dot(oda_ref[...], wda_ref[...]))
    o_ref[...] = x_ref[...] + _dot(mixed.astype(BF16), wout_ref[...])


def _merge(x2, o_sb, o_da, proj, w_sb, w_da, w_out, tm):
    t = x2.shape[0]
    const = lambda i: (0, 0)
    return pl.pallas_call(
        _merge_kernel,
        grid=(t // tm,),
        in_specs=[
            pl.BlockSpec((tm, D_MODEL), lambda i: (i, 0)),
            pl.BlockSpec((tm, SB_WIDTH), lambda i: (i, 0)),
            pl.BlockSpec((tm, DA_V_WIDTH), lambda i: (i, 0)),
            pl.BlockSpec((tm, D_MODEL), lambda i: (i, _GATE)),
            pl.BlockSpec((tm, D_MODEL), lambda i: (i, _GATE + 1)),
            pl.BlockSpec((SB_WIDTH, D_MODEL), const),
            pl.BlockSpec((DA_V_WIDTH, D_MODEL), const),
            pl.BlockSpec((D_MODEL, D_MODEL), const),
        ],
        out_specs=pl.BlockSpec((tm, D_MODEL), lambda i: (i, 0)),
        out_shape=jax.ShapeDtypeStruct((t, D_MODEL), F32),
        compiler_params=pltpu.CompilerParams(
            dimension_semantics=("arbitrary",), vmem_limit_bytes=VMEM_LIMIT),
        name="merge",
    )(x2, o_sb, o_da, proj, proj, w_sb, w_da, w_out)


def _ffn_kernel(x_ref, g_ref, wg_ref, wu_ref, wd_ref, gf_ref, o_ref, *, final_norm):
    x = x_ref[...]
    h = _rms(x, g_ref[...], NORM_EPS).astype(BF16)
    a = (jax.nn.silu(_dot(h, wg_ref[...])) * _dot(h, wu_ref[...])).astype(BF16)
    y = x + _dot(a, wd_ref[...])
    if final_norm:
        y = _rms(y, gf_ref[...], NORM_EPS)
    o_ref[...] = y


def _ffn(x2, g, w_gate, w_up, w_down, g_final, final_norm, tm):
    t = x2.shape[0]
    const = lambda i: (0, 0)
    return pl.pallas_call(
        functools.partial(_ffn_kernel, final_norm=final_norm),
        grid=(t // tm,),
        in_specs=[
            pl.BlockSpec((tm, D_MODEL), lambda i: (i, 0)),
            pl.BlockSpec((1, D_MODEL), const),
            pl.BlockSpec((D_MODEL, D_FF), const),
            pl.BlockSpec((D_MODEL, D_FF), const),
            pl.BlockSpec((D_FF, D_MODEL), const),
            pl.BlockSpec((1, D_MODEL), const),
        ],
        out_specs=pl.BlockSpec((tm, D_MODEL), lambda i: (i, 0)),
        out_shape=jax.ShapeDtypeStruct((t, D_MODEL), F32),
        compiler_params=pltpu.CompilerParams(
            dimension_semantics=("arbitrary",), vmem_limit_bytes=VMEM_LIMIT),
        name="ffn",
    )(x2, g, w_gate, w_up, w_down, g_final)


def _rope_tables(seq_len):
    inv_freq = 1.0 / (ROPE_THETA ** (jnp.arange(0, HEAD_DIM, 2, dtype=F32) / HEAD_DIM))
    ang = jnp.arange(seq_len, dtype=F32)[:, None] * inv_freq[None, :]
    ang = jnp.concatenate([ang, ang], axis=-1)
    sign = jnp.where(jnp.arange(HEAD_DIM) < HEAD_DIM // 2, -1.0, 1.0).astype(F32)
    cos = jnp.tile(jnp.cos(ang), (1, LANES // HEAD_DIM))
    sin = jnp.tile(jnp.sin(ang) * sign[None, :], (1, LANES // HEAD_DIM))
    return cos, sin


def kernel(x, g_mix, w_in, lambda_q1, lambda_k1, lambda_q2, lambda_k2, g_subln, w_branch_sb, w_branch_da,
           w_out, g_ffn, w_ffn_gate, w_ffn_up, w_ffn_down, g_final):
    b, s, d = x.shape
    depth = w_in.shape[0]
    assert d == D_MODEL and s % ATTN_TILE == 0
    tm = min(256, s)
    cos, sin = _rope_tables(s)
    x2 = x.reshape(b * s, d)
    for l in range(depth):
        lambda_init = 0.8 - 0.6 * math.exp(-0.3 * l)
        proj = _in_proj(x2, g_mix[l][None], w_in[l].astype(BF16), cos, sin, s, tm)
        proj3 = proj.reshape(b, s, IN_WIDTH)
        o_sb = _sb_attention(proj3)
        lam_params = jnp.stack([lambda_q1[l], lambda_k1[l], lambda_q2[l], lambda_k2[l]]).astype(F32)
        o_da = _da_attention(proj3, lam_params, g_subln[l][None], lambda_init)
        x2 = _merge(x2, o_sb.reshape(b * s, SB_WIDTH), o_da.reshape(b * s, DA_V_WIDTH), proj,
                    w_branch_sb[l].astype(BF16), w_branch_da[l].astype(BF16), w_out[l].astype(BF16), tm)
        x2 = _ffn(x2, g_ffn[l][None], w_ffn_gate[l].astype(BF16), w_ffn_up[l].astype(BF16),
                  w_ffn_down[l].astype(BF16), g_final[None], l == depth - 1, tm)
    return x2.reshape(b, s, d)
```

```python
import functools
import math

import jax
import jax.numpy as jnp
from jax import lax
from jax.experimental import pallas as pl
from jax.experimental.pallas import tpu as pltpu

D_MODEL = 1024
HEAD_DIM = 64
CHUNK = 64
SB_WIDTH = 512
DA_QK_WIDTH = 512
DA_V_WIDTH = 512
DA_HEADS = 4
N_BRANCH = 2
IN_WIDTH = 3 * SB_WIDTH + 2 * DA_QK_WIDTH + DA_V_WIDTH + N_BRANCH * D_MODEL
D_FF = 2816
ROPE_THETA = 10000.0
NORM_EPS = 1e-6
SUBLN_EPS = 1e-5
QK_SCALE = HEAD_DIM ** -0.5
LOG2E = math.log2(math.e)
EXP2_CLAMP = 126.0

LANES = 128
COL_TILE = 512
TOKEN_TILE = 512
ATTN_TILE = 256
SB_PHASE = 4
DA_UNROLL = 8
VMEM_LIMIT = 56 * 1024 * 1024

_QA, _KA, _VA = 0, 4, 8
_QD, _KD, _VD = 12, 16, 20
_GATE = 3

F32 = jnp.float32
BF16 = jnp.bfloat16


def _rms(x, g, eps):
    return x * lax.rsqrt(jnp.mean(x * x, axis=-1, keepdims=True) + eps) * g


def _nt_dot(a, b):
    return lax.dot_general(a, b, (((1,), (1,)), ((), ())), preferred_element_type=F32)


def _dot(a, b):
    return jnp.dot(a, b, preferred_element_type=F32)


def _in_proj_kernel(x_ref, g_ref, w_ref, cos_ref, sin_ref, o_ref):
    h = _rms(x_ref[...], g_ref[...], NORM_EPS).astype(BF16)
    tm = h.shape[0]
    lane = lax.broadcasted_iota(jnp.int32, (tm, LANES), 1)
    first_half = (lane % HEAD_DIM) < (HEAD_DIM // 2)
    cos = cos_ref[...]
    sin = sin_ref[...]

    def rope(a):
        outs = []
        for c in range(a.shape[1] // LANES):
            xs = a[:, c * LANES:(c + 1) * LANES]
            rot = jnp.where(first_half, pltpu.roll(xs, LANES - HEAD_DIM // 2, 1),
                            pltpu.roll(xs, HEAD_DIM // 2, 1))
            outs.append(xs * cos + rot * sin)
        return jnp.concatenate(outs, axis=1)

    for j in range(IN_WIDTH // COL_TILE):
        cols = slice(j * COL_TILE, (j + 1) * COL_TILE)
        a = _dot(h, w_ref[:, cols])
        if j == 0:
            a = a * (QK_SCALE * LOG2E)
        elif j == 3:
            a = rope(a) * (QK_SCALE * LOG2E)
        elif j == 4:
            a = rope(a)
        elif j >= 6:
            a = jax.nn.sigmoid(a)
        o_ref[:, cols] = a.astype(o_ref.dtype)


def _in_proj(x2, g, w, cos, sin, seq, tm):
    t = x2.shape[0]
    pos_blocks = seq // tm
    return pl.pallas_call(
        _in_proj_kernel,
        grid=(t // tm,),
        in_specs=[
            pl.BlockSpec((tm, D_MODEL), lambda i: (i, 0)),
            pl.BlockSpec((1, D_MODEL), lambda i: (0, 0)),
            pl.BlockSpec((D_MODEL, IN_WIDTH), lambda i: (0, 0)),
            pl.BlockSpec((tm, LANES), lambda i: (i % pos_blocks, 0)),
            pl.BlockSpec((tm, LANES), lambda i: (i % pos_blocks, 0)),
        ],
        out_specs=pl.BlockSpec((tm, IN_WIDTH), lambda i: (i, 0)),
        out_shape=jax.ShapeDtypeStruct((t, IN_WIDTH), BF16),
        compiler_params=pltpu.CompilerParams(
            dimension_semantics=("arbitrary",), vmem_limit_bytes=VMEM_LIMIT),
        name="in_proj",
    )(x2, g, w, cos, sin)


def _tile_schedule(n_tiles):
    steps = [(i, i) for i in range(n_tiles)]
    steps += [(i, j) for i in range(n_tiles) for j in range(i - 1, -1, -1)]
    return steps


def _sb_kernel(qi_tab, kj_tab, q_ref, k_ref, v_ref, o_ref, qs, vm, u2, racc, oacc,
               z_sl, hl_sl, rs_sl, w_sl, sc_sl, *, n_steps, n_diag):
    t = ATTN_TILE
    s_len = q_ref.shape[1]
    lane = lax.broadcasted_iota(jnp.int32, (t, LANES), 1)
    head0 = lane < HEAD_DIM

    def prep(c, carry):
        rows = pl.ds(pl.multiple_of(c * t, t), t)
        qc = q_ref[0, rows, :]
        vc = v_ref[0, rows, :]
        zero = jnp.zeros_like(qc)
        qs[0, rows, :] = jnp.where(head0, qc, zero)
        qs[1, rows, :] = jnp.where(head0, zero, qc)
        vm[0, rows, :] = jnp.where(head0, vc, zero)
        vm[1, rows, :] = jnp.where(head0, zero, vc)
        oacc[rows, :] = jnp.zeros((t, LANES), F32)
        racc[0, rows, :] = jnp.zeros((t, LANES), F32)
        racc[1, rows, :] = jnp.zeros((t, LANES), F32)
        return carry

    lax.fori_loop(0, s_len // t, prep, 0)
    row = lax.broadcasted_iota(jnp.int32, (t, t), 0)
    col = lax.broadcasted_iota(jnp.int32, (t, t), 1)
    u2[...] = jnp.where(row > col, 1.0, 0.0).astype(BF16)

    def rows_of(tab, s):
        return pl.ds(pl.multiple_of(tab[s] * t, t), t)

    def stage_a(s, z_s, hl, rs_s, masked):
        qrows = rows_of(qi_tab, s)
        q2 = jnp.concatenate([qs[0, qrows, :], qs[1, qrows, :]], axis=0)
        z = _nt_dot(q2, k_ref[0, rows_of(kj_tab, s), :])
        sp = jnp.maximum(z, jnp.log2(1.0 + jnp.exp2(jnp.minimum(z, EXP2_CLAMP))))
        own = z - sp
        if masked:
            r2 = lax.broadcasted_iota(jnp.int32, (2 * t, t), 0)
            c2 = lax.broadcasted_iota(jnp.int32, (2 * t, t), 1)
            earlier = c2 < (r2 % t)
            sp = jnp.where(earlier, sp, 0.0)
            own = jnp.where(earlier, own, -jnp.inf)
        z_s[...] = own
        hl[...] = sp.astype(BF16)
        rs_s[...] = jnp.broadcast_to(jnp.sum(sp, axis=1, keepdims=True), (2 * t, LANES))

    def stage_b(s, z_s, hl, rs_s, w, sc):
        wv = jnp.exp2(z_s[...] - _dot(hl[...], u2[...])).astype(BF16)
        w[:, 0:t] = wv[:t]
        w[:, t:2 * t] = wv[t:]
        qrows = rows_of(qi_tab, s)
        rs = rs_s[...]
        r0 = racc[0, qrows, :]
        r1 = racc[1, qrows, :]
        racc[0, qrows, :] = r0 + rs[:t]
        racc[1, qrows, :] = r1 + rs[t:]
        sc[...] = jnp.exp2(-jnp.where(head0, r0, r1))

    def stage_c(s, w, sc):
        krows = rows_of(kj_tab, s)
        v2 = jnp.concatenate([vm[0, krows, :], vm[1, krows, :]], axis=0)
        oacc[rows_of(qi_tab, s), :] += _dot(w[...], v2) * sc[...]

    ph = SB_PHASE

    def phase(base, a_set, masked, do_a=True, do_b=True, do_c=True):
        b_set = 1 - a_set
        for u in range(ph):
            if do_a:
                stage_a(base + u, z_sl.at[a_set, u], hl_sl.at[a_set, u], rs_sl.at[a_set, u], masked)
            if do_b:
                stage_b(base - ph + u, z_sl.at[b_set, u], hl_sl.at[b_set, u], rs_sl.at[b_set, u],
                        w_sl.at[b_set, u], sc_sl.at[b_set, u])
            if do_c:
                stage_c(base - 2 * ph + u, w_sl.at[a_set, u], sc_sl.at[a_set, u])

    def two_phases(masked):
        def body(j, carry):
            phase(2 * ph * j, 0, masked)
            phase(2 * ph * j + ph, 1, masked)
            return carry
        return body

    phase(0, 0, True, do_b=False, do_c=False)
    phase(ph, 1, True, do_c=False)
    lax.fori_loop(1, n_diag // (2 * ph), two_phases(True), 0)
    lax.fori_loop(n_diag // (2 * ph), n_steps // (2 * ph), two_phases(False), 0)
    phase(n_steps, 0, False, do_a=False)
    phase(n_steps + ph, 1, False, do_a=False, do_b=False)
    o_ref[0] = oacc[...].astype(o_ref.dtype)


def _sb_attention(proj3):
    b, s, _ = proj3.shape
    t = ATTN_TILE
    n_tiles = s // t
    steps = _tile_schedule(n_tiles)
    assert n_tiles % (2 * SB_PHASE) == 0 and len(steps) % (2 * SB_PHASE) == 0
    qi_tab = jnp.asarray([q for q, _ in steps], jnp.int32)
    kj_tab = jnp.asarray([k for _, k in steps], jnp.int32)
    col_block = lambda off: pl.BlockSpec((1, s, LANES), lambda bi, p, *_: (bi, 0, off + p))
    return pl.pallas_call(
        functools.partial(_sb_kernel, n_steps=len(steps), n_diag=n_tiles),
        grid_spec=pltpu.PrefetchScalarGridSpec(
            num_scalar_prefetch=2,
            grid=(b, SB_WIDTH // LANES),
            in_specs=[col_block(_QA), col_block(_KA), col_block(_VA)],
            out_specs=pl.BlockSpec((1, s, LANES), lambda bi, p, *_: (bi, 0, p)),
            scratch_shapes=[
                pltpu.VMEM((2, s, LANES), BF16),
                pltpu.VMEM((2, s, LANES), BF16),
                pltpu.VMEM((t, t), BF16),
                pltpu.VMEM((2, s, LANES), F32),
                pltpu.VMEM((s, LANES), F32),
                pltpu.VMEM((2, SB_PHASE, 2 * t, t), F32),
                pltpu.VMEM((2, SB_PHASE, 2 * t, t), BF16),
                pltpu.VMEM((2, SB_PHASE, 2 * t, LANES), F32),
                pltpu.VMEM((2, SB_PHASE, t, 2 * t), BF16),
                pltpu.VMEM((2, SB_PHASE, t, LANES), F32),
            ]),
        out_shape=jax.ShapeDtypeStruct((b, s, SB_WIDTH), BF16),
        compiler_params=pltpu.CompilerParams(
            dimension_semantics=("arbitrary", "arbitrary"), vmem_limit_bytes=VMEM_LIMIT),
        name="sb_attn",
    )(qi_tab, kj_tab, proj3, proj3, proj3)


def _da_kernel(qi_tab, kj_tab, lam_ref, g_ref, q_ref, k_ref, v_ref, o_ref, qs, acc, macc, lacc,
               p0, p1, al0, al1, *, lambda_init, n_steps, n_diag):
    t = ATTN_TILE
    s_len = q_ref.shape[1]
    lane = lax.broadcasted_iota(jnp.int32, (t, LANES), 1)
    map0 = lane < HEAD_DIM

    def prep(c, carry):
        rows = pl.ds(pl.multiple_of(c * t, t), t)
        qc = q_ref[0, rows, :]
        zero = jnp.zeros_like(qc)
        qs[0, rows, :] = jnp.where(map0, qc, zero)
        qs[1, rows, :] = jnp.where(map0, zero, qc)
        for c2 in range(2):
            acc[c2, rows, :] = jnp.zeros((t, LANES), F32)
            lacc[c2, rows, :] = jnp.zeros((t, LANES), F32)
            macc[c2, rows, :] = jnp.full((t, LANES), -jnp.inf, F32)
        return carry

    lax.fori_loop(0, s_len // t, prep, 0)

    def rows_of(tab, s):
        return pl.ds(pl.multiple_of(tab[s] * t, t), t)

    def stage_a(s, p_s, al_s, masked):
        qrows = rows_of(qi_tab, s)
        q2 = jnp.concatenate([qs[0, qrows, :], qs[1, qrows, :]], axis=0)
        sc = _nt_dot(q2, k_ref[0, rows_of(kj_tab, s), :])
        if masked:
            r2 = lax.broadcasted_iota(jnp.int32, (2 * t, t), 0)
            c2 = lax.broadcasted_iota(jnp.int32, (2 * t, t), 1)
            sc = jnp.where((c2 // CHUNK) <= ((r2 % t) // CHUNK), sc, -jnp.inf)
        m_prev = jnp.concatenate([macc[0, qrows, :], macc[1, qrows, :]], axis=0)
        m_new = jnp.maximum(m_prev, jnp.max(sc, axis=1, keepdims=True))
        p = jnp.exp2(sc - jnp.concatenate([m_new] * (t // LANES), axis=1))
        alpha = jnp.exp2(m_prev - m_new)
        l_prev = jnp.concatenate([lacc[0, qrows, :], lacc[1, qrows, :]], axis=0)
        l_new = alpha * l_prev + jnp.sum(p, axis=1, keepdims=True)
        for c in range(2):
            macc[c, qrows, :] = m_new[c * t:(c + 1) * t]
            lacc[c, qrows, :] = l_new[c * t:(c + 1) * t]
        p_s[...] = p.astype(BF16)
        al_s[...] = alpha

    def stage_b(s, p_s, al_s):
        qrows = rows_of(qi_tab, s)
        pv = _dot(p_s[...], v_ref[0, rows_of(kj_tab, s), :])
        al = al_s[...]
        for c in range(2):
            acc[c, qrows, :] = al[c * t:(c + 1) * t] * acc[c, qrows, :] + pv[c * t:(c + 1) * t]

    slot = ((p0, al0), (p1, al1))

    def step(s, parity, masked):
        stage_a(s, *slot[parity], masked)
        stage_b(s - 1, *slot[1 - parity])

    def steps_per_trip(masked):
        def body(i, carry):
            for u in range(DA_UNROLL):
                step(DA_UNROLL * i + u, u % 2, masked)
            return carry
        return body

    stage_a(0, p0, al0, True)
    for u in range(1, DA_UNROLL):
        step(u, u % 2, True)
    lax.fori_loop(1, n_diag // DA_UNROLL, steps_per_trip(True), 0)
    lax.fori_loop(n_diag // DA_UNROLL, n_steps // DA_UNROLL, steps_per_trip(False), 0)
    stage_b(n_steps - 1, *slot[(n_steps - 1) % 2])

    lam_p = lam_ref[...]
    lam = (jnp.exp(jnp.sum(lam_p[0:1] * lam_p[1:2], axis=1, keepdims=True))
           - jnp.exp(jnp.sum(lam_p[2:3] * lam_p[3:4], axis=1, keepdims=True)) + lambda_init)
    g = g_ref[...]

    def finish(c, carry):
        rows = pl.ds(pl.multiple_of(c * t, t), t)
        o = acc[0, rows, :] / lacc[0, rows, :] - lam * (acc[1, rows, :] / lacc[1, rows, :])
        o_ref[0, rows, :] = (_rms(o, g, SUBLN_EPS) * (1.0 - lambda_init)).astype(o_ref.dtype)
        return carry

    lax.fori_loop(0, s_len // t, finish, 0)


def _da_attention(proj3, lam_params, g_subln, lambda_init):
    b, s, _ = proj3.shape
    t = ATTN_TILE
    n_tiles = s // t
    steps = _tile_schedule(n_tiles)
    assert n_tiles % DA_UNROLL == 0 and len(steps) % DA_UNROLL == 0
    qi_tab = jnp.asarray([q for q, _ in steps], jnp.int32)
    kj_tab = jnp.asarray([k for _, k in steps], jnp.int32)
    col_block = lambda off: pl.BlockSpec((1, s, LANES), lambda bi, h, *_: (bi, 0, off + h))
    return pl.pallas_call(
        functools.partial(_da_kernel, lambda_init=lambda_init, n_steps=len(steps), n_diag=n_tiles),
        grid_spec=pltpu.PrefetchScalarGridSpec(
            num_scalar_prefetch=2,
            grid=(b, DA_HEADS),
            in_specs=[
                pl.BlockSpec((4, HEAD_DIM), lambda bi, h, *_: (0, 0)),
                pl.BlockSpec((1, LANES), lambda bi, h, *_: (0, 0)),
                col_block(_QD), col_block(_KD), col_block(_VD),
            ],
            out_specs=pl.BlockSpec((1, s, LANES), lambda bi, h, *_: (bi, 0, h)),
            scratch_shapes=[
                pltpu.VMEM((2, s, LANES), BF16),
                pltpu.VMEM((2, s, LANES), F32),
                pltpu.VMEM((2, s, LANES), F32),
                pltpu.VMEM((2, s, LANES), F32),
                pltpu.VMEM((2 * t, t), BF16), pltpu.VMEM((2 * t, t), BF16),
                pltpu.VMEM((2 * t, LANES), F32), pltpu.VMEM((2 * t, LANES), F32),
            ]),
        out_shape=jax.ShapeDtypeStruct((b, s, DA_V_WIDTH), BF16),
        compiler_params=pltpu.CompilerParams(
            dimension_semantics=("arbitrary", "arbitrary"), vmem_limit_bytes=VMEM_LIMIT),
        name="da_attn",
    )(qi_tab, kj_tab, lam_params, g_subln, proj3, proj3, proj3)


def _mix_ffn_kernel(x_ref, osb_ref, oda_ref, gsb_ref, gda_ref, wsb_ref, wda_ref, wout_ref,
                    g_ref, wg_ref, wu_ref, wd_ref, gf_ref, o_ref, *, final_norm):
    mixed = (gsb_ref[...].astype(F32) * _dot(osb_ref[...], wsb_ref[...])
             + gda_ref[...].astype(F32) * _dot(oda_ref[...], wda_ref[...]))
    x = x_ref[...] + _dot(mixed.astype(BF16), wout_ref[...])
    h = _rms(x, g_ref[...], NORM_EPS).astype(BF16)
    a = (jax.nn.silu(_dot(h, wg_ref[...])) * _dot(h, wu_ref[...])).astype(BF16)
    y = x + _dot(a, wd_ref[...])
    if final_norm:
        y = _rms(y, gf_ref[...], NORM_EPS)
    o_ref[...] = y


def _mix_ffn(x2, o_sb, o_da, proj, w_sb, w_da, w_out, g, w_gate, w_up, w_down, g_final, final_norm, tm):
    t = x2.shape[0]
    rows = lambda width, col=0: pl.BlockSpec((tm, width), lambda i: (i, col))
    whole = lambda shape: pl.BlockSpec(shape, lambda i: (0, 0), pipeline_mode=pl.Buffered(1))
    return pl.pallas_call(
        functools.partial(_mix_ffn_kernel, final_norm=final_norm),
        grid=(t // tm,),
        in_specs=[
            rows(D_MODEL), rows(SB_WIDTH), rows(DA_V_WIDTH), rows(D_MODEL, _GATE), rows(D_MODEL, _GATE + 1),
            whole((SB_WIDTH, D_MODEL)), whole((DA_V_WIDTH, D_MODEL)), whole((D_MODEL, D_MODEL)),
            whole((1, D_MODEL)), whole((D_MODEL, D_FF)), whole((D_MODEL, D_FF)), whole((D_FF, D_MODEL)),
            whole((1, D_MODEL)),
        ],
        out_specs=rows(D_MODEL),
        out_shape=jax.ShapeDtypeStruct((t, D_MODEL), F32),
        compiler_params=pltpu.CompilerParams(
            dimension_semantics=("arbitrary",), vmem_limit_bytes=VMEM_LIMIT),
        name="mix_ffn",
    )(x2, o_sb, o_da, proj, proj, w_sb, w_da, w_out, g, w_gate, w_up, w_down, g_final)


def _rope_tables(seq_len):
    inv_freq = 1.0 / (ROPE_THETA ** (jnp.arange(0, HEAD_DIM, 2, dtype=F32) / HEAD_DIM))
    ang = jnp.arange(seq_len, dtype=F32)[:, None] * inv_freq[None, :]
    ang = jnp.concatenate([ang, ang], axis=-1)
    sign = jnp.where(jnp.arange(HEAD_DIM) < HEAD_DIM // 2, -1.0, 1.0).astype(F32)
    cos = jnp.tile(jnp.cos(ang), (1, LANES // HEAD_DIM))
    sin = jnp.tile(jnp.sin(ang) * sign[None, :], (1, LANES // HEAD_DIM))
    return cos, sin


def kernel(x, g_mix, w_in, lambda_q1, lambda_k1, lambda_q2, lambda_k2, g_subln, w_branch_sb, w_branch_da,
           w_out, g_ffn, w_ffn_gate, w_ffn_up, w_ffn_down, g_final):
    b, s, d = x.shape
    depth = w_in.shape[0]
    assert d == D_MODEL and s % ATTN_TILE == 0
    tm = min(TOKEN_TILE, s)
    cos, sin = _rope_tables(s)
    x2 = x.reshape(b * s, d)
    for l in range(depth):
        lambda_init = 0.8 - 0.6 * math.exp(-0.3 * l)
        proj = _in_proj(x2, g_mix[l][None], w_in[l].astype(BF16), cos, sin, s, tm)
        proj3 = proj.reshape(b, s, IN_WIDTH)
        o_sb = _sb_attention(proj3)
        lam_params = jnp.stack([lambda_q1[l], lambda_k1[l], lambda_q2[l], lambda_k2[l]]).astype(F32)
        o_da = _da_attention(proj3, lam_params, g_subln[l][None], lambda_init)
        x2 = _mix_ffn(x2, o_sb.reshape(b * s, SB_WIDTH), o_da.reshape(b * s, DA_V_WIDTH), proj,
                      w_branch_sb[l].astype(BF16), w_branch_da[l].astype(BF16), w_out[l].astype(BF16),
                      g_ffn[l][None], w_ffn_gate[l].astype(BF16), w_ffn_up[l].astype(BF16),
                      w_ffn_down[l].astype(BF16), g_final[None], l == depth - 1, tm)
    return x2.reshape(b, s, d)
```

```python
import functools
import math

import jax
import jax.numpy as jnp
from jax import lax
from jax.experimental import pallas as pl
from jax.experimental.pallas import tpu as pltpu

D_MODEL = 1024
HEAD_DIM = 64
CHUNK = 64
SB_WIDTH = 512
DA_QK_WIDTH = 512
DA_V_WIDTH = 512
DA_HEADS = 4
N_BRANCH = 2
IN_WIDTH = 3 * SB_WIDTH + 2 * DA_QK_WIDTH + DA_V_WIDTH + N_BRANCH * D_MODEL
D_FF = 2816
ROPE_THETA = 10000.0
NORM_EPS = 1e-6
SUBLN_EPS = 1e-5
QK_SCALE = HEAD_DIM ** -0.5
LOG2E = math.log2(math.e)
EXP2_CLAMP = 126.0

LANES = 128
COL_TILE = 512
TOKEN_TILE = 512
ATTN_TILE = 256
SB_PHASE = 4
DA_UNROLL = 8
VMEM_LIMIT = 56 * 1024 * 1024

_QA, _KA, _VA = 0, 4, 8
_QD, _KD, _VD = 12, 16, 20
_GATE = 3

F32 = jnp.float32
BF16 = jnp.bfloat16


def _rms(x, g, eps):
    return x * lax.rsqrt(jnp.mean(x * x, axis=-1, keepdims=True) + eps) * g


def _nt_dot(a, b):
    return lax.dot_general(a, b, (((1,), (1,)), ((), ())), preferred_element_type=F32)


def _dot(a, b):
    return jnp.dot(a, b, preferred_element_type=F32)


def _in_proj_kernel(x_ref, g_ref, w_ref, cos_ref, sin_ref, o_ref):
    h = _rms(x_ref[...], g_ref[...], NORM_EPS).astype(BF16)
    tm = h.shape[0]
    lane = lax.broadcasted_iota(jnp.int32, (tm, LANES), 1)
    first_half = (lane % HEAD_DIM) < (HEAD_DIM // 2)
    cos = cos_ref[...]
    sin = sin_ref[...]

    def rope(a):
        outs = []
        for c in range(a.shape[1] // LANES):
            xs = a[:, c * LANES:(c + 1) * LANES]
            rot = jnp.where(first_half, pltpu.roll(xs, LANES - HEAD_DIM // 2, 1),
                            pltpu.roll(xs, HEAD_DIM // 2, 1))
            outs.append(xs * cos + rot * sin)
        return jnp.concatenate(outs, axis=1)

    for j in range(IN_WIDTH // COL_TILE):
        cols = slice(j * COL_TILE, (j + 1) * COL_TILE)
        a = _dot(h, w_ref[:, cols])
        if j == 0:
            a = a * (QK_SCALE * LOG2E)
        elif j == 3:
            a = rope(a) * (QK_SCALE * LOG2E)
        elif j == 4:
            a = rope(a)
        elif j >= 6:
            a = jax.nn.sigmoid(a)
        o_ref[:, cols] = a.astype(o_ref.dtype)


def _in_proj(x2, g, w, cos, sin, seq, tm):
    t = x2.shape[0]
    pos_blocks = seq // tm
    return pl.pallas_call(
        _in_proj_kernel,
        grid=(t // tm,),
        in_specs=[
            pl.BlockSpec((tm, D_MODEL), lambda i: (i, 0)),
            pl.BlockSpec((1, D_MODEL), lambda i: (0, 0)),
            pl.BlockSpec((D_MODEL, IN_WIDTH), lambda i: (0, 0)),
            pl.BlockSpec((tm, LANES), lambda i: (i % pos_blocks, 0)),
            pl.BlockSpec((tm, LANES), lambda i: (i % pos_blocks, 0)),
        ],
        out_specs=pl.BlockSpec((tm, IN_WIDTH), lambda i: (i, 0)),
        out_shape=jax.ShapeDtypeStruct((t, IN_WIDTH), BF16),
        compiler_params=pltpu.CompilerParams(
            dimension_semantics=("arbitrary",), vmem_limit_bytes=VMEM_LIMIT),
        name="in_proj",
    )(x2, g, w, cos, sin)


def _tile_schedule(n_tiles):
    steps = [(i, i) for i in range(n_tiles)]
    steps += [(i, j) for i in range(n_tiles) for j in range(i - 1, -1, -1)]
    return steps


def _sb_kernel(qi_tab, kj_tab, q_ref, k_ref, v_ref, o_ref, qs, vm, u2, racc, oacc,
               z_sl, hl_sl, rs_sl, w_sl, sc_sl, *, n_steps, n_diag):
    t = ATTN_TILE
    s_len = q_ref.shape[1]
    lane = lax.broadcasted_iota(jnp.int32, (t, LANES), 1)
    head0 = lane < HEAD_DIM

    def prep(c, carry):
        rows = pl.ds(pl.multiple_of(c * t, t), t)
        qc = q_ref[0, rows, :]
        vc = v_ref[0, rows, :]
        zero = jnp.zeros_like(qc)
        qs[0, rows, :] = jnp.where(head0, qc, zero)
        qs[1, rows, :] = jnp.where(head0, zero, qc)
        vm[0, rows, :] = jnp.where(head0, vc, zero)
        vm[1, rows, :] = jnp.where(head0, zero, vc)
        oacc[rows, :] = jnp.zeros((t, LANES), F32)
        racc[0, rows, :] = jnp.zeros((t, LANES), F32)
        racc[1, rows, :] = jnp.zeros((t, LANES), F32)
        return carry

    lax.fori_loop(0, s_len // t, prep, 0)
    row = lax.broadcasted_iota(jnp.int32, (t, t), 0)
    col = lax.broadcasted_iota(jnp.int32, (t, t), 1)
    u2[...] = jnp.where(row > col, 1.0, 0.0).astype(BF16)

    def rows_of(tab, s):
        return pl.ds(pl.multiple_of(tab[s] * t, t), t)

    def stage_a(s, z_s, hl, rs_s, masked):
        qrows = rows_of(qi_tab, s)
        q2 = jnp.concatenate([qs[0, qrows, :], qs[1, qrows, :]], axis=0)
        z = _nt_dot(q2, k_ref[0, rows_of(kj_tab, s), :])
        sp = jnp.maximum(z, jnp.log2(1.0 + jnp.exp2(jnp.minimum(z, EXP2_CLAMP))))
        own = z - sp
        if masked:
            r2 = lax.broadcasted_iota(jnp.int32, (2 * t, t), 0)
            c2 = lax.broadcasted_iota(jnp.int32, (2 * t, t), 1)
            earlier = c2 < (r2 % t)
            sp = jnp.where(earlier, sp, 0.0)
            own = jnp.where(earlier, own, -jnp.inf)
        z_s[...] = own
        hl[...] = sp.astype(BF16)
        rs_s[...] = jnp.broadcast_to(jnp.sum(sp, axis=1, keepdims=True), (2 * t, LANES))

    def stage_b(s, z_s, hl, rs_s, w, sc):
        wv = jnp.exp2(z_s[...] - _dot(hl[...], u2[...])).astype(BF16)
        w[:, 0:t] = wv[:t]
        w[:, t:2 * t] = wv[t:]
        qrows = rows_of(qi_tab, s)
        rs = rs_s[...]
        r0 = racc[0, qrows, :]
        r1 = racc[1, qrows, :]
        racc[0, qrows, :] = r0 + rs[:t]
        racc[1, qrows, :] = r1 + rs[t:]
        sc[...] = jnp.exp2(-jnp.where(head0, r0, r1))

    def stage_c(s, w, sc):
        krows = rows_of(kj_tab, s)
        v2 = jnp.concatenate([vm[0, krows, :], vm[1, krows, :]], axis=0)
        oacc[rows_of(qi_tab, s), :] += _dot(w[...], v2) * sc[...]

    ph = SB_PHASE

    def phase(base, a_set, masked, do_a=True, do_b=True, do_c=True):
        b_set = 1 - a_set
        for u in range(ph):
            if do_a:
                stage_a(base + u, z_sl.at[a_set, u], hl_sl.at[a_set, u], rs_sl.at[a_set, u], masked)
            if do_b:
                stage_b(base - ph + u, z_sl.at[b_set, u], hl_sl.at[b_set, u], rs_sl.at[b_set, u],
                        w_sl.at[b_set, u], sc_sl.at[b_set, u])
            if do_c:
                stage_c(base - 2 * ph + u, w_sl.at[a_set, u], sc_sl.at[a_set, u])

    def two_phases(masked):
        def body(j, carry):
            phase(2 * ph * j, 0, masked)
            phase(2 * ph * j + ph, 1, masked)
            return carry
        return body

    phase(0, 0, True, do_b=False, do_c=False)
    phase(ph, 1, True, do_c=False)
    lax.fori_loop(1, n_diag // (2 * ph), two_phases(True), 0)
    lax.fori_loop(n_diag // (2 * ph), n_steps // (2 * ph), two_phases(False), 0)
    phase(n_steps, 0, False, do_a=False)
    phase(n_steps + ph, 1, False, do_a=False, do_b=False)
    o_ref[0] = oacc[...].astype(o_ref.dtype)


def _sb_attention(proj3):
    b, s, _ = proj3.shape
    t = ATTN_TILE
    n_tiles = s // t
    steps = _tile_schedule(n_tiles)
    assert n_tiles % (2 * SB_PHASE) == 0 and len(steps) % (2 * SB_PHASE) == 0
    qi_tab = jnp.asarray([q for q, _ in steps], jnp.int32)
    kj_tab = jnp.asarray([k for _, k in steps], jnp.int32)
    col_block = lambda off: pl.BlockSpec((1, s, LANES), lambda bi, p, *_: (bi, 0, off + p))
    return pl.pallas_call(
        functools.partial(_sb_kernel, n_steps=len(steps), n_diag=n_tiles),
        grid_spec=pltpu.PrefetchScalarGridSpec(
            num_scalar_prefetch=2,
            grid=(b, SB_WIDTH // LANES),
            in_specs=[col_block(_QA), col_block(_KA), col_block(_VA)],
            out_specs=pl.BlockSpec((1, s, LANES), lambda bi, p, *_: (bi, 0, p)),
            scratch_shapes=[
                pltpu.VMEM((2, s, LANES), BF16),
                pltpu.VMEM((2, s, LANES), BF16),
                pltpu.VMEM((t, t), BF16),
                pltpu.VMEM((2, s, LANES), F32),
                pltpu.VMEM((s, LANES), F32),
                pltpu.VMEM((2, SB_PHASE, 2 * t, t), F32),
                pltpu.VMEM((2, SB_PHASE, 2 * t, t), BF16),
                pltpu.VMEM((2, SB_PHASE, 2 * t, LANES), F32),
                pltpu.VMEM((2, SB_PHASE, t, 2 * t), BF16),
                pltpu.VMEM((2, SB_PHASE, t, LANES), F32),
            ]),
        out_shape=jax.ShapeDtypeStruct((b, s, SB_WIDTH), BF16),
        compiler_params=pltpu.CompilerParams(
            dimension_semantics=("arbitrary", "arbitrary"), vmem_limit_bytes=VMEM_LIMIT),
        name="sb_attn",
    )(qi_tab, kj_tab, proj3, proj3, proj3)


def _da_kernel(qi_tab, kj_tab, lam_ref, g_ref, q_ref, k_ref, v_ref, o_ref, qs, vx, acc, macc,
               p0, p1, al0, al1, *, lambda_init, n_steps, n_diag):
    t = ATTN_TILE
    s_len = q_ref.shape[1]
    lane = lax.broadcasted_iota(jnp.int32, (t, LANES), 1)
    map0 = lane < HEAD_DIM

    def prep(c, carry):
        rows = pl.ds(pl.multiple_of(c * t, t), t)
        qc = q_ref[0, rows, :]
        zero = jnp.zeros_like(qc)
        qs[0, rows, :] = jnp.where(map0, qc, zero)
        qs[1, rows, :] = jnp.where(map0, zero, qc)
        vx[rows, 0:LANES] = v_ref[0, rows, :]
        vx[rows, LANES:2 * LANES] = jnp.ones((t, LANES), BF16)
        for c2 in range(2):
            acc[c2, rows, :] = jnp.zeros((t, 2 * LANES), F32)
            macc[c2, rows, :] = jnp.full((t, LANES), -jnp.inf, F32)
        return carry

    lax.fori_loop(0, s_len // t, prep, 0)

    def rows_of(tab, s):
        return pl.ds(pl.multiple_of(tab[s] * t, t), t)

    def stage_a(s, p_s, al_s, masked):
        qrows = rows_of(qi_tab, s)
        q2 = jnp.concatenate([qs[0, qrows, :], qs[1, qrows, :]], axis=0)
        sc = _nt_dot(q2, k_ref[0, rows_of(kj_tab, s), :])
        if masked:
            r2 = lax.broadcasted_iota(jnp.int32, (2 * t, t), 0)
            c2 = lax.broadcasted_iota(jnp.int32, (2 * t, t), 1)
            sc = jnp.where((c2 // CHUNK) <= ((r2 % t) // CHUNK), sc, -jnp.inf)
        m_prev = jnp.concatenate([macc[0, qrows, :], macc[1, qrows, :]], axis=0)
        m_new = jnp.maximum(m_prev, jnp.max(sc, axis=1, keepdims=True))
        p_s[...] = jnp.exp2((sc - jnp.concatenate([m_new] * (t // LANES), axis=1)).astype(BF16))
        al_s[...] = jnp.exp2(m_prev - m_new)
        for c in range(2):
            macc[c, qrows, :] = m_new[c * t:(c + 1) * t]

    def stage_b(s, p_s, al_s):
        qrows = rows_of(qi_tab, s)
        pv = _dot(p_s[...], vx[rows_of(kj_tab, s), :])
        al = al_s[...]
        al = jnp.concatenate([al, al], axis=1)
        for c in range(2):
            acc[c, qrows, :] = al[c * t:(c + 1) * t] * acc[c, qrows, :] + pv[c * t:(c + 1) * t]

    slot = ((p0, al0), (p1, al1))

    def step(s, parity, masked):
        stage_a(s, *slot[parity], masked)
        stage_b(s - 1, *slot[1 - parity])

    def steps_per_trip(masked):
        def body(i, carry):
            for u in range(DA_UNROLL):
                step(DA_UNROLL * i + u, u % 2, masked)
            return carry
        return body

    stage_a(0, p0, al0, True)
    for u in range(1, DA_UNROLL):
        step(u, u % 2, True)
    lax.fori_loop(1, n_diag // DA_UNROLL, steps_per_trip(True), 0)
    lax.fori_loop(n_diag // DA_UNROLL, n_steps // DA_UNROLL, steps_per_trip(False), 0)
    stage_b(n_steps - 1, *slot[(n_steps - 1) % 2])

    lam_p = lam_ref[...]
    lam = (jnp.exp(jnp.sum(lam_p[0:1] * lam_p[1:2], axis=1, keepdims=True))
           - jnp.exp(jnp.sum(lam_p[2:3] * lam_p[3:4], axis=1, keepdims=True)) + lambda_init)
    g = g_ref[...]

    def finish(c, carry):
        rows = pl.ds(pl.multiple_of(c * t, t), t)
        a0 = acc[0, rows, :]
        a1 = acc[1, rows, :]
        o = a0[:, :LANES] / a0[:, LANES:] - lam * (a1[:, :LANES] / a1[:, LANES:])
        o_ref[0, rows, :] = (_rms(o, g, SUBLN_EPS) * (1.0 - lambda_init)).astype(o_ref.dtype)
        return carry

    lax.fori_loop(0, s_len // t, finish, 0)


def _da_attention(proj3, lam_params, g_subln, lambda_init):
    b, s, _ = proj3.shape
    t = ATTN_TILE
    n_tiles = s // t
    steps = _tile_schedule(n_tiles)
    assert n_tiles % DA_UNROLL == 0 and len(steps) % DA_UNROLL == 0
    qi_tab = jnp.asarray([q for q, _ in steps], jnp.int32)
    kj_tab = jnp.asarray([k for _, k in steps], jnp.int32)
    col_block = lambda off: pl.BlockSpec((1, s, LANES), lambda bi, h, *_: (bi, 0, off + h))
    return pl.pallas_call(
        functools.partial(_da_kernel, lambda_init=lambda_init, n_steps=len(steps), n_diag=n_tiles),
        grid_spec=pltpu.PrefetchScalarGridSpec(
            num_scalar_prefetch=2,
            grid=(b, DA_HEADS),
            in_specs=[
                pl.BlockSpec((4, HEAD_DIM), lambda bi, h, *_: (0, 0)),
                pl.BlockSpec((1, LANES), lambda bi, h, *_: (0, 0)),
                col_block(_QD), col_block(_KD), col_block(_VD),
            ],
            out_specs=pl.BlockSpec((1, s, LANES), lambda bi, h, *_: (bi, 0, h)),
            scratch_shapes=[
                pltpu.VMEM((2, s, LANES), BF16),
                pltpu.VMEM((s, 2 * LANES), BF16),
                pltpu.VMEM((2, s, 2 * LANES), F32),
                pltpu.VMEM((2, s, LANES), F32),
                pltpu.VMEM((2 * t, t), BF16), pltpu.VMEM((2 * t, t), BF16),
                pltpu.VMEM((2 * t, LANES), F32), pltpu.VMEM((2 * t, LANES), F32),
            ]),
        out_shape=jax.ShapeDtypeStruct((b, s, DA_V_WIDTH), BF16),
        compiler_params=pltpu.CompilerParams(
            dimension_semantics=("arbitrary", "arbitrary"), vmem_limit_bytes=VMEM_LIMIT),
        name="da_attn",
    )(qi_tab, kj_tab, lam_params, g_subln, proj3, proj3, proj3)


def _mix_ffn_kernel(x_ref, osb_ref, oda_ref, gsb_ref, gda_ref, wsb_ref, wda_ref, wout_ref,
                    g_ref, wg_ref, wu_ref, wd_ref, gf_ref, o_ref, *, final_norm):
    mixed = (gsb_ref[...].astype(F32) * _dot(osb_ref[...], wsb_ref[...])
             + gda_ref[...].astype(F32) * _dot(oda_ref[...], wda_ref[...]))
    x = x_ref[...] + _dot(mixed.astype(BF16), wout_ref[...])
    h = _rms(x, g_ref[...], NORM_EPS).astype(BF16)
    a = (jax.nn.silu(_dot(h, wg_ref[...])) * _dot(h, wu_ref[...])).astype(BF16)
    y = x + _dot(a, wd_ref[...])
    if final_norm:
        y = _rms(y, gf_ref[...], NORM_EPS)
    o_ref[...] = y


def _mix_ffn(x2, o_sb, o_da, proj, w_sb, w_da, w_out, g, w_gate, w_up, w_down, g_final, final_norm, tm):
    t = x2.shape[0]
    rows = lambda width, col=0: pl.BlockSpec((tm, width), lambda i: (i, col))
    whole = lambda shape: pl.BlockSpec(shape, lambda i: (0, 0), pipeline_mode=pl.Buffered(1))
    return pl.pallas_call(
        functools.partial(_mix_ffn_kernel, final_norm=final_norm),
        grid=(t // tm,),
        in_specs=[
            rows(D_MODEL), rows(SB_WIDTH), rows(DA_V_WIDTH), rows(D_MODEL, _GATE), rows(D_MODEL, _GATE + 1),
            whole((SB_WIDTH, D_MODEL)), whole((DA_V_WIDTH, D_MODEL)), whole((D_MODEL, D_MODEL)),
            whole((1, D_MODEL)), whole((D_MODEL, D_FF)), whole((D_MODEL, D_FF)), whole((D_FF, D_MODEL)),
            whole((1, D_MODEL)),
        ],
        out_specs=rows(D_MODEL),
        out_shape=jax.ShapeDtypeStruct((t, D_MODEL), F32),
        compiler_params=pltpu.CompilerParams(
            dimension_semantics=("arbitrary",), vmem_limit_bytes=VMEM_LIMIT),
        name="mix_ffn",
    )(x2, o_sb, o_da, proj, proj, w_sb, w_da, w_out, g, w_gate, w_up, w_down, g_final)


def _rope_tables(seq_len):
    inv_freq = 1.0 / (ROPE_THETA ** (jnp.arange(0, HEAD_DIM, 2, dtype=F32) / HEAD_DIM))
    ang = jnp.arange(seq_len, dtype=F32)[:, None] * inv_freq[None, :]
    ang = jnp.concatenate([ang, ang], axis=-1)
    sign = jnp.where(jnp.arange(HEAD_DIM) < HEAD_DIM // 2, -1.0, 1.0).astype(F32)
    cos = jnp.tile(jnp.cos(ang), (1, LANES // HEAD_DIM))
    sin = jnp.tile(jnp.sin(ang) * sign[None, :], (1, LANES // HEAD_DIM))
    return cos, sin


def kernel(x, g_mix, w_in, lambda_q1, lambda_k1, lambda_q2, lambda_k2, g_subln, w_branch_sb, w_branch_da,
           w_out, g_ffn, w_ffn_gate, w_ffn_up, w_ffn_down, g_final):
    b, s, d = x.shape
    depth = w_in.shape[0]
    assert d == D_MODEL and s % ATTN_TILE == 0
    tm = min(TOKEN_TILE, s)
    cos, sin = _rope_tables(s)
    x2 = x.reshape(b * s, d)
    for l in range(depth):
        lambda_init = 0.8 - 0.6 * math.exp(-0.3 * l)
        proj = _in_proj(x2, g_mix[l][None], w_in[l].astype(BF16), cos, sin, s, tm)
        proj3 = proj.reshape(b, s, IN_WIDTH)
        o_sb = _sb_attention(proj3)
        lam_params = jnp.stack([lambda_q1[l], lambda_k1[l], lambda_q2[l], lambda_k2[l]]).astype(F32)
        o_da = _da_attention(proj3, lam_params, g_subln[l][None], lambda_init)
        x2 = _mix_ffn(x2, o_sb.reshape(b * s, SB_WIDTH), o_da.reshape(b * s, DA_V_WIDTH), proj,
                      w_branch_sb[l].astype(BF16), w_branch_da[l].astype(BF16), w_out[l].astype(BF16),
                      g_ffn[l][None], w_ffn_gate[l].astype(BF16), w_ffn_up[l].astype(BF16),
                      w_ffn_down[l].astype(BF16), g_final[None], l == depth - 1, tm)
    return x2.reshape(b, s, d)
```

```python
import functools
import math

import jax
import jax.numpy as jnp
from jax import lax
from jax.experimental import pallas as pl
from jax.experimental.pallas import tpu as pltpu

D_MODEL = 1024
HEAD_DIM = 64
CHUNK = 64
SB_WIDTH = 512
DA_QK_WIDTH = 512
DA_V_WIDTH = 512
DA_HEADS = 4
N_BRANCH = 2
IN_WIDTH = 3 * SB_WIDTH + 2 * DA_QK_WIDTH + DA_V_WIDTH + N_BRANCH * D_MODEL
D_FF = 2816
ROPE_THETA = 10000.0
NORM_EPS = 1e-6
SUBLN_EPS = 1e-5
QK_SCALE = HEAD_DIM ** -0.5
LOG2E = math.log2(math.e)
EXP2_CLAMP = 126.0

LANES = 128
COL_TILE = 512
TOKEN_TILE = 512
ATTN_TILE = 256
SB_PHASE = 4
DA_UNROLL = 8
VMEM_LIMIT = 56 * 1024 * 1024

_QA, _KA, _VA = 0, 4, 8
_QD, _KD, _VD = 12, 16, 20
_GATE = 3

F32 = jnp.float32
BF16 = jnp.bfloat16


def _rms(x, g, eps):
    return x * lax.rsqrt(jnp.mean(x * x, axis=-1, keepdims=True) + eps) * g


def _nt_dot(a, b):
    return lax.dot_general(a, b, (((1,), (1,)), ((), ())), preferred_element_type=F32)


def _dot(a, b):
    return jnp.dot(a, b, preferred_element_type=F32)


def _in_proj_kernel(x_ref, g_ref, w_ref, cos_ref, sin_ref, o_ref):
    h = _rms(x_ref[...], g_ref[...], NORM_EPS).astype(BF16)
    tm = h.shape[0]
    lane = lax.broadcasted_iota(jnp.int32, (tm, LANES), 1)
    first_half = (lane % HEAD_DIM) < (HEAD_DIM // 2)
    cos = cos_ref[...]
    sin = sin_ref[...]

    def rope(a):
        outs = []
        for c in range(a.shape[1] // LANES):
            xs = a[:, c * LANES:(c + 1) * LANES]
            rot = jnp.where(first_half, pltpu.roll(xs, LANES - HEAD_DIM // 2, 1),
                            pltpu.roll(xs, HEAD_DIM // 2, 1))
            outs.append(xs * cos + rot * sin)
        return jnp.concatenate(outs, axis=1)

    for j in range(IN_WIDTH // COL_TILE):
        cols = slice(j * COL_TILE, (j + 1) * COL_TILE)
        a = _dot(h, w_ref[:, cols])
        if j == 0:
            a = a * (QK_SCALE * LOG2E)
        elif j == 3:
            a = rope(a) * (QK_SCALE * LOG2E)
        elif j == 4:
            a = rope(a)
        elif j >= 6:
            a = jax.nn.sigmoid(a)
        o_ref[:, cols] = a.astype(o_ref.dtype)


def _in_proj(x2, g, w, cos, sin, seq, tm):
    t = x2.shape[0]
    pos_blocks = seq // tm
    return pl.pallas_call(
        _in_proj_kernel,
        grid=(t // tm,),
        in_specs=[
            pl.BlockSpec((tm, D_MODEL), lambda i: (i, 0)),
            pl.BlockSpec((1, D_MODEL), lambda i: (0, 0)),
            pl.BlockSpec((D_MODEL, IN_WIDTH), lambda i: (0, 0)),
            pl.BlockSpec((tm, LANES), lambda i: (i % pos_blocks, 0)),
            pl.BlockSpec((tm, LANES), lambda i: (i % pos_blocks, 0)),
        ],
        out_specs=pl.BlockSpec((tm, IN_WIDTH), lambda i: (i, 0)),
        out_shape=jax.ShapeDtypeStruct((t, IN_WIDTH), BF16),
        compiler_params=pltpu.CompilerParams(
            dimension_semantics=("arbitrary",), vmem_limit_bytes=VMEM_LIMIT),
        name="in_proj",
    )(x2, g, w, cos, sin)


def _tile_schedule(n_tiles):
    steps = [(i, i) for i in range(n_tiles)]
    steps += [(i, j) for i in range(n_tiles) for j in range(i - 1, -1, -1)]
    return steps


def _sb_kernel(qi_tab, kj_tab, q_ref, k_ref, v_ref, o_ref, qs, vm, u2, racc, oacc,
               z_sl, hl_sl, rs_sl, w_sl, sc_sl, *, n_steps, n_diag):
    t = ATTN_TILE
    s_len = q_ref.shape[1]
    lane = lax.broadcasted_iota(jnp.int32, (t, LANES), 1)
    head0 = lane < HEAD_DIM

    def prep(c, carry):
        rows = pl.ds(pl.multiple_of(c * t, t), t)
        qc = q_ref[0, rows, :]
        vc = v_ref[0, rows, :]
        zero = jnp.zeros_like(qc)
        qs[0, rows, :] = jnp.where(head0, qc, zero)
        qs[1, rows, :] = jnp.where(head0, zero, qc)
        vm[0, rows, :] = jnp.where(head0, vc, zero)
        vm[1, rows, :] = jnp.where(head0, zero, vc)
        oacc[rows, :] = jnp.zeros((t, LANES), F32)
        racc[0, rows, :] = jnp.zeros((t, LANES), F32)
        racc[1, rows, :] = jnp.zeros((t, LANES), F32)
        return carry

    lax.fori_loop(0, s_len // t, prep, 0)
    row = lax.broadcasted_iota(jnp.int32, (t, t), 0)
    col = lax.broadcasted_iota(jnp.int32, (t, t), 1)
    u2[...] = jnp.where(row > col, 1.0, 0.0).astype(BF16)

    def rows_of(tab, s):
        return pl.ds(pl.multiple_of(tab[s] * t, t), t)

    def stage_a(s, z_s, hl, rs_s, masked):
        qrows = rows_of(qi_tab, s)
        q2 = jnp.concatenate([qs[0, qrows, :], qs[1, qrows, :]], axis=0)
        z = _nt_dot(q2, k_ref[0, rows_of(kj_tab, s), :])
        sp = jnp.maximum(z, jnp.log2(1.0 + jnp.exp2(jnp.minimum(z, EXP2_CLAMP))))
        own = z - sp
        if masked:
            r2 = lax.broadcasted_iota(jnp.int32, (2 * t, t), 0)
            c2 = lax.broadcasted_iota(jnp.int32, (2 * t, t), 1)
            earlier = c2 < (r2 % t)
            sp = jnp.where(earlier, sp, 0.0)
            own = jnp.where(earlier, own, -jnp.inf)
        z_s[...] = own
        hl[...] = sp.astype(BF16)
        rs_s[...] = jnp.broadcast_to(jnp.sum(sp, axis=1, keepdims=True), (2 * t, LANES))

    def stage_b(s, z_s, hl, rs_s, w, sc):
        wv = jnp.exp2((z_s[...] - _dot(hl[...], u2[...])).astype(BF16))
        w[:, 0:t] = wv[:t]
        w[:, t:2 * t] = wv[t:]
        qrows = rows_of(qi_tab, s)
        rs = rs_s[...]
        r0 = racc[0, qrows, :]
        r1 = racc[1, qrows, :]
        racc[0, qrows, :] = r0 + rs[:t]
        racc[1, qrows, :] = r1 + rs[t:]
        sc[...] = jnp.exp2(-jnp.where(head0, r0, r1))

    def stage_c(s, w, sc):
        krows = rows_of(kj_tab, s)
        v2 = jnp.concatenate([vm[0, krows, :], vm[1, krows, :]], axis=0)
        oacc[rows_of(qi_tab, s), :] += _dot(w[...], v2) * sc[...]

    ph = SB_PHASE

    def phase(base, a_set, masked, do_a=True, do_b=True, do_c=True):
        b_set = 1 - a_set
        for u in range(ph):
            if do_a:
                stage_a(base + u, z_sl.at[a_set, u], hl_sl.at[a_set, u], rs_sl.at[a_set, u], masked)
            if do_b:
                stage_b(base - ph + u, z_sl.at[b_set, u], hl_sl.at[b_set, u], rs_sl.at[b_set, u],
                        w_sl.at[b_set, u], sc_sl.at[b_set, u])
            if do_c:
                stage_c(base - 2 * ph + u, w_sl.at[a_set, u], sc_sl.at[a_set, u])

    def two_phases(masked):
        def body(j, carry):
            phase(2 * ph * j, 0, masked)
            phase(2 * ph * j + ph, 1, masked)
            return carry
        return body

    phase(0, 0, True, do_b=False, do_c=False)
    phase(ph, 1, True, do_c=False)
    lax.fori_loop(1, n_diag // (2 * ph), two_phases(True), 0)
    lax.fori_loop(n_diag // (2 * ph), n_steps // (2 * ph), two_phases(False), 0)
    phase(n_steps, 0, False, do_a=False)
    phase(n_steps + ph, 1, False, do_a=False, do_b=False)
    o_ref[0] = oacc[...].astype(o_ref.dtype)


def _sb_attention(proj3):
    b, s, _ = proj3.shape
    t = ATTN_TILE
    n_tiles = s // t
    steps = _tile_schedule(n_tiles)
    assert n_tiles % (2 * SB_PHASE) == 0 and len(steps) % (2 * SB_PHASE) == 0
    qi_tab = jnp.asarray([q for q, _ in steps], jnp.int32)
    kj_tab = jnp.asarray([k for _, k in steps], jnp.int32)
    col_block = lambda off: pl.BlockSpec((1, s, LANES), lambda bi, p, *_: (bi, 0, off + p))
    return pl.pallas_call(
        functools.partial(_sb_kernel, n_steps=len(steps), n_diag=n_tiles),
        grid_spec=pltpu.PrefetchScalarGridSpec(
            num_scalar_prefetch=2,
            grid=(b, SB_WIDTH // LANES),
            in_specs=[col_block(_QA), col_block(_KA), col_block(_VA)],
            out_specs=pl.BlockSpec((1, s, LANES), lambda bi, p, *_: (bi, 0, p)),
            scratch_shapes=[
                pltpu.VMEM((2, s, LANES), BF16),
                pltpu.VMEM((2, s, LANES), BF16),
                pltpu.VMEM((t, t), BF16),
                pltpu.VMEM((2, s, LANES), F32),
                pltpu.VMEM((s, LANES), F32),
                pltpu.VMEM((2, SB_PHASE, 2 * t, t), F32),
                pltpu.VMEM((2, SB_PHASE, 2 * t, t), BF16),
                pltpu.VMEM((2, SB_PHASE, 2 * t, LANES), F32),
                pltpu.VMEM((2, SB_PHASE, t, 2 * t), BF16),
                pltpu.VMEM((2, SB_PHASE, t, LANES), F32),
            ]),
        out_shape=jax.ShapeDtypeStruct((b, s, SB_WIDTH), BF16),
        compiler_params=pltpu.CompilerParams(
            dimension_semantics=("arbitrary", "arbitrary"), vmem_limit_bytes=VMEM_LIMIT),
        name="sb_attn",
    )(qi_tab, kj_tab, proj3, proj3, proj3)


def _da_kernel(qi_tab, kj_tab, lam_ref, g_ref, q_ref, k_ref, v_ref, o_ref, qs, vx, acc, macc,
               p0, p1, al0, al1, *, lambda_init, n_steps, n_diag):
    t = ATTN_TILE
    s_len = q_ref.shape[1]
    lane = lax.broadcasted_iota(jnp.int32, (t, LANES), 1)
    map0 = lane < HEAD_DIM

    def prep(c, carry):
        rows = pl.ds(pl.multiple_of(c * t, t), t)
        qc = q_ref[0, rows, :]
        zero = jnp.zeros_like(qc)
        qs[0, rows, :] = jnp.where(map0, qc, zero)
        qs[1, rows, :] = jnp.where(map0, zero, qc)
        vx[rows, 0:LANES] = v_ref[0, rows, :]
        vx[rows, LANES:2 * LANES] = jnp.ones((t, LANES), BF16)
        for c2 in range(2):
            acc[c2, rows, :] = jnp.zeros((t, 2 * LANES), F32)
            macc[c2, rows, :] = jnp.full((t, LANES), -jnp.inf, F32)
        return carry

    lax.fori_loop(0, s_len // t, prep, 0)

    def rows_of(tab, s):
        return pl.ds(pl.multiple_of(tab[s] * t, t), t)

    def stage_a(s, p_s, al_s, masked):
        qrows = rows_of(qi_tab, s)
        q2 = jnp.concatenate([qs[0, qrows, :], qs[1, qrows, :]], axis=0)
        sc = _nt_dot(q2, k_ref[0, rows_of(kj_tab, s), :])
        if masked:
            r2 = lax.broadcasted_iota(jnp.int32, (2 * t, t), 0)
            c2 = lax.broadcasted_iota(jnp.int32, (2 * t, t), 1)
            sc = jnp.where((c2 // CHUNK) <= ((r2 % t) // CHUNK), sc, -jnp.inf)
        m_prev = jnp.concatenate([macc[0, qrows, :], macc[1, qrows, :]], axis=0)
        m_new = jnp.maximum(m_prev, jnp.max(sc, axis=1, keepdims=True))
        p_s[...] = jnp.exp2((sc - jnp.concatenate([m_new] * (t // LANES), axis=1)).astype(BF16))
        al_s[...] = jnp.exp2(m_prev - m_new)
        for c in range(2):
            macc[c, qrows, :] = m_new[c * t:(c + 1) * t]

    def stage_b(s, p_s, al_s):
        qrows = rows_of(qi_tab, s)
        pv = _dot(p_s[...], vx[rows_of(kj_tab, s), :])
        al = al_s[...]
        al = jnp.concatenate([al, al], axis=1)
        for c in range(2):
            acc[c, qrows, :] = al[c * t:(c + 1) * t] * acc[c, qrows, :] + pv[c * t:(c + 1) * t]

    slot = ((p0, al0), (p1, al1))

    def step(s, parity, masked):
        stage_a(s, *slot[parity], masked)
        stage_b(s - 1, *slot[1 - parity])

    def steps_per_trip(masked):
        def body(i, carry):
            for u in range(DA_UNROLL):
                step(DA_UNROLL * i + u, u % 2, masked)
            return carry
        return body

    stage_a(0, p0, al0, True)
    for u in range(1, DA_UNROLL):
        step(u, u % 2, True)
    lax.fori_loop(1, n_diag // DA_UNROLL, steps_per_trip(True), 0)
    lax.fori_loop(n_diag // DA_UNROLL, n_steps // DA_UNROLL, steps_per_trip(False), 0)
    stage_b(n_steps - 1, *slot[(n_steps - 1) % 2])

    lam_p = lam_ref[...]
    lam = (jnp.exp(jnp.sum(lam_p[0:1] * lam_p[1:2], axis=1, keepdims=True))
           - jnp.exp(jnp.sum(lam_p[2:3] * lam_p[3:4], axis=1, keepdims=True)) + lambda_init)
    g = g_ref[...]

    def finish(c, carry):
        rows = pl.ds(pl.multiple_of(c * t, t), t)
        a0 = acc[0, rows, :]
        a1 = acc[1, rows, :]
        o = a0[:, :LANES] / a0[:, LANES:] - lam * (a1[:, :LANES] / a1[:, LANES:])
        o_ref[0, rows, :] = (_rms(o, g, SUBLN_EPS) * (1.0 - lambda_init)).astype(o_ref.dtype)
        return carry

    lax.fori_loop(0, s_len // t, finish, 0)


def _da_attention(proj3, lam_params, g_subln, lambda_init):
    b, s, _ = proj3.shape
    t = ATTN_TILE
    n_tiles = s // t
    steps = _tile_schedule(n_tiles)
    assert n_tiles % DA_UNROLL == 0 and len(steps) % DA_UNROLL == 0
    qi_tab = jnp.asarray([q for q, _ in steps], jnp.int32)
    kj_tab = jnp.asarray([k for _, k in steps], jnp.int32)
    col_block = lambda off: pl.BlockSpec((1, s, LANES), lambda bi, h, *_: (bi, 0, off + h))
    return pl.pallas_call(
        functools.partial(_da_kernel, lambda_init=lambda_init, n_steps=len(steps), n_diag=n_tiles),
        grid_spec=pltpu.PrefetchScalarGridSpec(
            num_scalar_prefetch=2,
            grid=(b, DA_HEADS),
            in_specs=[
                pl.BlockSpec((4, HEAD_DIM), lambda bi, h, *_: (0, 0)),
                pl.BlockSpec((1, LANES), lambda bi, h, *_: (0, 0)),
                col_block(_QD), col_block(_KD), col_block(_VD),
            ],
            out_specs=pl.BlockSpec((1, s, LANES), lambda bi, h, *_: (bi, 0, h)),
            scratch_shapes=[
                pltpu.VMEM((2, s, LANES), BF16),
                pltpu.VMEM((s, 2 * LANES), BF16),
                pltpu.VMEM((2, s, 2 * LANES), F32),
                pltpu.VMEM((2, s, LANES), F32),
                pltpu.VMEM((2 * t, t), BF16), pltpu.VMEM((2 * t, t), BF16),
                pltpu.VMEM((2 * t, LANES), F32), pltpu.VMEM((2 * t, LANES), F32),
            ]),
        out_shape=jax.ShapeDtypeStruct((b, s, DA_V_WIDTH), BF16),
        compiler_params=pltpu.CompilerParams(
            dimension_semantics=("arbitrary", "arbitrary"), vmem_limit_bytes=VMEM_LIMIT),
        name="da_attn",
    )(qi_tab, kj_tab, lam_params, g_subln, proj3, proj3, proj3)


def _mix_ffn_kernel(x_ref, osb_ref, oda_ref, gsb_ref, gda_ref, wsb_ref, wda_ref, wout_ref,
                    g_ref, wg_ref, wu_ref, wd_ref, gf_ref, o_ref, *, final_norm):
    mixed = (gsb_ref[...].astype(F32) * _dot(osb_ref[...], wsb_ref[...])
             + gda_ref[...].astype(F32) * _dot(oda_ref[...], wda_ref[...]))
    x = x_ref[...] + _dot(mixed.astype(BF16), wout_ref[...])
    h = _rms(x, g_ref[...], NORM_EPS).astype(BF16)
    a = (jax.nn.silu(_dot(h, wg_ref[...])) * _dot(h, wu_ref[...])).astype(BF16)
    y = x + _dot(a, wd_ref[...])
    if final_norm:
        y = _rms(y, gf_ref[...], NORM_EPS)
    o_ref[...] = y


def _mix_ffn(x2, o_sb, o_da, proj, w_sb, w_da, w_out, g, w_gate, w_up, w_down, g_final, final_norm, tm):
    t = x2.shape[0]
    rows = lambda width, col=0: pl.BlockSpec((tm, width), lambda i: (i, col))
    whole = lambda shape: pl.BlockSpec(shape, lambda i: (0, 0), pipeline_mode=pl.Buffered(1))
    return pl.pallas_call(
        functools.partial(_mix_ffn_kernel, final_norm=final_norm),
        grid=(t // tm,),
        in_specs=[
            rows(D_MODEL), rows(SB_WIDTH), rows(DA_V_WIDTH), rows(D_MODEL, _GATE), rows(D_MODEL, _GATE + 1),
            whole((SB_WIDTH, D_MODEL)), whole((DA_V_WIDTH, D_MODEL)), whole((D_MODEL, D_MODEL)),
            whole((1, D_MODEL)), whole((D_MODEL, D_FF)), whole((D_MODEL, D_FF)), whole((D_FF, D_MODEL)),
            whole((1, D_MODEL)),
        ],
        out_specs=rows(D_MODEL),
        out_shape=jax.ShapeDtypeStruct((t, D_MODEL), F32),
        compiler_params=pltpu.CompilerParams(
            dimension_semantics=("arbitrary",), vmem_limit_bytes=VMEM_LIMIT),
        name="mix_ffn",
    )(x2, o_sb, o_da, proj, proj, w_sb, w_da, w_out, g, w_gate, w_up, w_down, g_final)


def _rope_tables(seq_len):
    inv_freq = 1.0 / (ROPE_THETA ** (jnp.arange(0, HEAD_DIM, 2, dtype=F32) / HEAD_DIM))
    ang = jnp.arange(seq_len, dtype=F32)[:, None] * inv_freq[None, :]
    ang = jnp.concatenate([ang, ang], axis=-1)
    sign = jnp.where(jnp.arange(HEAD_DIM) < HEAD_DIM // 2, -1.0, 1.0).astype(F32)
    cos = jnp.tile(jnp.cos(ang), (1, LANES // HEAD_DIM))
    sin = jnp.tile(jnp.sin(ang) * sign[None, :], (1, LANES // HEAD_DIM))
    return cos, sin


def kernel(x, g_mix, w_in, lambda_q1, lambda_k1, lambda_q2, lambda_k2, g_subln, w_branch_sb, w_branch_da,
           w_out, g_ffn, w_ffn_gate, w_ffn_up, w_ffn_down, g_final):
    b, s, d = x.shape
    depth = w_in.shape[0]
    assert d == D_MODEL and s % ATTN_TILE == 0
    tm = min(TOKEN_TILE, s)
    cos, sin = _rope_tables(s)
    x2 = x.reshape(b * s, d)
    for l in range(depth):
        lambda_init = 0.8 - 0.6 * math.exp(-0.3 * l)
        proj = _in_proj(x2, g_mix[l][None], w_in[l].astype(BF16), cos, sin, s, tm)
        proj3 = proj.reshape(b, s, IN_WIDTH)
        o_sb = _sb_attention(proj3)
        lam_params = jnp.stack([lambda_q1[l], lambda_k1[l], lambda_q2[l], lambda_k2[l]]).astype(F32)
        o_da = _da_attention(proj3, lam_params, g_subln[l][None], lambda_init)
        x2 = _mix_ffn(x2, o_sb.reshape(b * s, SB_WIDTH), o_da.reshape(b * s, DA_V_WIDTH), proj,
                      w_branch_sb[l].astype(BF16), w_branch_da[l].astype(BF16), w_out[l].astype(BF16),
                      g_ffn[l][None], w_ffn_gate[l].astype(BF16), w_ffn_up[l].astype(BF16),
                      w_ffn_down[l].astype(BF16), g_final[None], l == depth - 1, tm)
    return x2.reshape(b, s, d)
```

```python
import functools
import math

import jax
import jax.numpy as jnp
from jax import lax
from jax.experimental import pallas as pl
from jax.experimental.pallas import tpu as pltpu

D_MODEL = 1024
HEAD_DIM = 64
CHUNK = 64
SB_WIDTH = 512
DA_QK_WIDTH = 512
DA_V_WIDTH = 512
DA_HEADS = 4
N_BRANCH = 2
IN_WIDTH = 3 * SB_WIDTH + 2 * DA_QK_WIDTH + DA_V_WIDTH + N_BRANCH * D_MODEL
D_FF = 2816
ROPE_THETA = 10000.0
NORM_EPS = 1e-6
SUBLN_EPS = 1e-5
QK_SCALE = HEAD_DIM ** -0.5
LOG2E = math.log2(math.e)
EXP2_CLAMP = 126.0

LANES = 128
COL_TILE = 512
TOKEN_TILE = 512
ATTN_TILE = 256
SB_PHASE = 4
DA_UNROLL = 8
VMEM_LIMIT = 56 * 1024 * 1024

_QA, _KA, _VA = 0, 4, 8
_QD, _KD, _VD = 12, 16, 20
_GATE = 3

F32 = jnp.float32
BF16 = jnp.bfloat16


def _rms(x, g, eps):
    return x * lax.rsqrt(jnp.mean(x * x, axis=-1, keepdims=True) + eps) * g


def _nt_dot(a, b):
    return lax.dot_general(a, b, (((1,), (1,)), ((), ())), preferred_element_type=F32)


def _dot(a, b):
    return jnp.dot(a, b, preferred_element_type=F32)


def _in_proj_kernel(x_ref, g_ref, w_ref, cos_ref, sin_ref, o_ref):
    h = _rms(x_ref[...], g_ref[...], NORM_EPS).astype(BF16)
    tm = h.shape[0]
    lane = lax.broadcasted_iota(jnp.int32, (tm, LANES), 1)
    first_half = (lane % HEAD_DIM) < (HEAD_DIM // 2)
    cos = cos_ref[...]
    sin = sin_ref[...]

    def rope(a):
        outs = []
        for c in range(a.shape[1] // LANES):
            xs = a[:, c * LANES:(c + 1) * LANES]
            rot = jnp.where(first_half, pltpu.roll(xs, LANES - HEAD_DIM // 2, 1),
                            pltpu.roll(xs, HEAD_DIM // 2, 1))
            outs.append(xs * cos + rot * sin)
        return jnp.concatenate(outs, axis=1)

    for j in range(IN_WIDTH // COL_TILE):
        cols = slice(j * COL_TILE, (j + 1) * COL_TILE)
        a = _dot(h, w_ref[:, cols])
        if j == 0:
            a = a * (QK_SCALE * LOG2E)
        elif j == 3:
            a = rope(a) * (QK_SCALE * LOG2E)
        elif j == 4:
            a = rope(a)
        elif j >= 6:
            a = jax.nn.sigmoid(a)
        o_ref[:, cols] = a.astype(o_ref.dtype)


def _in_proj(x2, g, w, cos, sin, seq, tm):
    t = x2.shape[0]
    pos_blocks = seq // tm
    return pl.pallas_call(
        _in_proj_kernel,
        grid=(t // tm,),
        in_specs=[
            pl.BlockSpec((tm, D_MODEL), lambda i: (i, 0)),
            pl.BlockSpec((1, D_MODEL), lambda i: (0, 0)),
            pl.BlockSpec((D_MODEL, IN_WIDTH), lambda i: (0, 0)),
            pl.BlockSpec((tm, LANES), lambda i: (i % pos_blocks, 0)),
            pl.BlockSpec((tm, LANES), lambda i: (i % pos_blocks, 0)),
        ],
        out_specs=pl.BlockSpec((tm, IN_WIDTH), lambda i: (i, 0)),
        out_shape=jax.ShapeDtypeStruct((t, IN_WIDTH), BF16),
        compiler_params=pltpu.CompilerParams(
            dimension_semantics=("arbitrary",), vmem_limit_bytes=VMEM_LIMIT),
        name="in_proj",
    )(x2, g, w, cos, sin)


def _tile_schedule(n_tiles):
    steps = [(i, i) for i in range(n_tiles)]
    steps += [(i, j) for i in range(n_tiles) for j in range(i - 1, -1, -1)]
    return steps


def _sb_kernel(qi_tab, kj_tab, q_ref, k_ref, v_ref, o_ref, qs, vm, u2, racc, oacc,
               z_sl, hl_sl, rs_sl, w_sl, sc_sl, *, n_steps, n_diag):
    t = ATTN_TILE
    s_len = q_ref.shape[1]
    lane = lax.broadcasted_iota(jnp.int32, (t, LANES), 1)
    head0 = lane < HEAD_DIM

    def prep(c, carry):
        rows = pl.ds(pl.multiple_of(c * t, t), t)
        qc = q_ref[0, rows, :]
        vc = v_ref[0, rows, :]
        zero = jnp.zeros_like(qc)
        qs[0, rows, :] = jnp.where(head0, qc, zero)
        qs[1, rows, :] = jnp.where(head0, zero, qc)
        vm[0, rows, :] = jnp.where(head0, vc, zero)
        vm[1, rows, :] = jnp.where(head0, zero, vc)
        oacc[rows, :] = jnp.zeros((t, LANES), F32)
        racc[0, rows, :] = jnp.zeros((t, LANES), F32)
        racc[1, rows, :] = jnp.zeros((t, LANES), F32)
        return carry

    lax.fori_loop(0, s_len // t, prep, 0)
    row = lax.broadcasted_iota(jnp.int32, (t, t), 0)
    col = lax.broadcasted_iota(jnp.int32, (t, t), 1)
    u2[...] = jnp.where(row > col, 1.0, 0.0).astype(BF16)

    def rows_of(tab, s):
        return pl.ds(pl.multiple_of(tab[s] * t, t), t)

    def stage_a(s, z_s, hl, rs_s, masked):
        qrows = rows_of(qi_tab, s)
        q2 = jnp.concatenate([qs[0, qrows, :], qs[1, qrows, :]], axis=0)
        z = _nt_dot(q2, k_ref[0, rows_of(kj_tab, s), :])
        sp = jnp.maximum(z, jnp.log2(1.0 + jnp.exp2(jnp.minimum(z, EXP2_CLAMP))))
        own = z - sp
        if masked:
            r2 = lax.broadcasted_iota(jnp.int32, (2 * t, t), 0)
            c2 = lax.broadcasted_iota(jnp.int32, (2 * t, t), 1)
            earlier = c2 < (r2 % t)
            sp = jnp.where(earlier, sp, 0.0)
            own = jnp.where(earlier, own, -jnp.inf)
        z_s[...] = own
        hl[...] = sp.astype(BF16)
        rs_s[...] = jnp.broadcast_to(jnp.sum(sp, axis=1, keepdims=True), (2 * t, LANES))

    def stage_b(s, z_s, hl, rs_s, w, sc):
        wv = jnp.exp2(z_s[...] - _dot(hl[...], u2[...])).astype(BF16)
        w[:, 0:t] = wv[:t]
        w[:, t:2 * t] = wv[t:]
        qrows = rows_of(qi_tab, s)
        rs = rs_s[...]
        r0 = racc[0, qrows, :]
        r1 = racc[1, qrows, :]
        racc[0, qrows, :] = r0 + rs[:t]
        racc[1, qrows, :] = r1 + rs[t:]
        sc[...] = jnp.exp2(-jnp.where(head0, r0, r1))

    def stage_c(s, w, sc):
        krows = rows_of(kj_tab, s)
        v2 = jnp.concatenate([vm[0, krows, :], vm[1, krows, :]], axis=0)
        oacc[rows_of(qi_tab, s), :] += _dot(w[...], v2) * sc[...]

    ph = SB_PHASE

    def phase(base, a_set, masked, do_a=True, do_b=True, do_c=True):
        b_set = 1 - a_set
        for u in range(ph):
            if do_a:
                stage_a(base + u, z_sl.at[a_set, u], hl_sl.at[a_set, u], rs_sl.at[a_set, u], masked)
            if do_b:
                stage_b(base - ph + u, z_sl.at[b_set, u], hl_sl.at[b_set, u], rs_sl.at[b_set, u],
                        w_sl.at[b_set, u], sc_sl.at[b_set, u])
            if do_c:
                stage_c(base - 2 * ph + u, w_sl.at[a_set, u], sc_sl.at[a_set, u])

    def two_phases(masked):
        def body(j, carry):
            phase(2 * ph * j, 0, masked)
            phase(2 * ph * j + ph, 1, masked)
            return carry
        return body

    phase(0, 0, True, do_b=False, do_c=False)
    phase(ph, 1, True, do_c=False)
    lax.fori_loop(1, n_diag // (2 * ph), two_phases(True), 0)
    lax.fori_loop(n_diag // (2 * ph), n_steps // (2 * ph), two_phases(False), 0)
    phase(n_steps, 0, False, do_a=False)
    phase(n_steps + ph, 1, False, do_a=False, do_b=False)
    o_ref[0] = oacc[...].astype(o_ref.dtype)


def _sb_attention(proj3):
    b, s, _ = proj3.shape
    t = ATTN_TILE
    n_tiles = s // t
    steps = _tile_schedule(n_tiles)
    assert n_tiles % (2 * SB_PHASE) == 0 and len(steps) % (2 * SB_PHASE) == 0
    qi_tab = jnp.asarray([q for q, _ in steps], jnp.int32)
    kj_tab = jnp.asarray([k for _, k in steps], jnp.int32)
    col_block = lambda off: pl.BlockSpec((1, s, LANES), lambda bi, p, *_: (bi, 0, off + p))
    return pl.pallas_call(
        functools.partial(_sb_kernel, n_steps=len(steps), n_diag=n_tiles),
        grid_spec=pltpu.PrefetchScalarGridSpec(
            num_scalar_prefetch=2,
            grid=(b, SB_WIDTH // LANES),
            in_specs=[col_block(_QA), col_block(_KA), col_block(_VA)],
            out_specs=pl.BlockSpec((1, s, LANES), lambda bi, p, *_: (bi, 0, p)),
            scratch_shapes=[
                pltpu.VMEM((2, s, LANES), BF16),
                pltpu.VMEM((2, s, LANES), BF16),
                pltpu.VMEM((t, t), BF16),
                pltpu.VMEM((2, s, LANES), F32),
                pltpu.VMEM((s, LANES), F32),
                pltpu.VMEM((2, SB_PHASE, 2 * t, t), F32),
                pltpu.VMEM((2, SB_PHASE, 2 * t, t), BF16),
                pltpu.VMEM((2, SB_PHASE, 2 * t, LANES), F32),
                pltpu.VMEM((2, SB_PHASE, t, 2 * t), BF16),
                pltpu.VMEM((2, SB_PHASE, t, LANES), F32),
            ]),
        out_shape=jax.ShapeDtypeStruct((b, s, SB_WIDTH), BF16),
        compiler_params=pltpu.CompilerParams(
            dimension_semantics=("arbitrary", "arbitrary"), vmem_limit_bytes=VMEM_LIMIT),
        name="sb_attn",
    )(qi_tab, kj_tab, proj3, proj3, proj3)


def _da_kernel(qi_tab, kj_tab, lam_ref, g_ref, q_ref, k_ref, v_ref, o_ref, qs, vx, acc, macc,
               p0, p1, al0, al1, *, lambda_init, n_steps, n_diag):
    t = ATTN_TILE
    s_len = q_ref.shape[1]
    lane = lax.broadcasted_iota(jnp.int32, (t, LANES), 1)
    map0 = lane < HEAD_DIM

    def prep(c, carry):
        rows = pl.ds(pl.multiple_of(c * t, t), t)
        qc = q_ref[0, rows, :]
        zero = jnp.zeros_like(qc)
        qs[0, rows, :] = jnp.where(map0, qc, zero)
        qs[1, rows, :] = jnp.where(map0, zero, qc)
        vx[rows, 0:LANES] = v_ref[0, rows, :]
        vx[rows, LANES:2 * LANES] = jnp.ones((t, LANES), BF16)
        for c2 in range(2):
            acc[c2, rows, :] = jnp.zeros((t, 2 * LANES), F32)
            macc[c2, rows, :] = jnp.full((t, LANES), -jnp.inf, F32)
        return carry

    lax.fori_loop(0, s_len // t, prep, 0)

    def rows_of(tab, s):
        return pl.ds(pl.multiple_of(tab[s] * t, t), t)

    def stage_a(s, p_s, al_s, masked):
        qrows = rows_of(qi_tab, s)
        q2 = jnp.concatenate([qs[0, qrows, :], qs[1, qrows, :]], axis=0)
        sc = _nt_dot(q2, k_ref[0, rows_of(kj_tab, s), :])
        if masked:
            r2 = lax.broadcasted_iota(jnp.int32, (2 * t, t), 0)
            c2 = lax.broadcasted_iota(jnp.int32, (2 * t, t), 1)
            sc = jnp.where((c2 // CHUNK) <= ((r2 % t) // CHUNK), sc, -jnp.inf)
        m_prev = jnp.concatenate([macc[0, qrows, :], macc[1, qrows, :]], axis=0)
        m_new = jnp.maximum(m_prev, jnp.max(sc, axis=1, keepdims=True))
        p_s[...] = jnp.exp2(sc - jnp.concatenate([m_new] * (t // LANES), axis=1)).astype(BF16)
        al_s[...] = jnp.exp2(m_prev - m_new)
        for c in range(2):
            macc[c, qrows, :] = m_new[c * t:(c + 1) * t]

    def stage_b(s, p_s, al_s):
        qrows = rows_of(qi_tab, s)
        pv = _dot(p_s[...], vx[rows_of(kj_tab, s), :])
        al = al_s[...]
        al = jnp.concatenate([al, al], axis=1)
        for c in range(2):
            acc[c, qrows, :] = al[c * t:(c + 1) * t] * acc[c, qrows, :] + pv[c * t:(c + 1) * t]

    slot = ((p0, al0), (p1, al1))

    def step(s, parity, masked):
        stage_a(s, *slot[parity], masked)
        stage_b(s - 1, *slot[1 - parity])

    def steps_per_trip(masked):
        def body(i, carry):
            for u in range(DA_UNROLL):
                step(DA_UNROLL * i + u, u % 2, masked)
            return carry
        return body

    stage_a(0, p0, al0, True)
    for u in range(1, DA_UNROLL):
        step(u, u % 2, True)
    lax.fori_loop(1, n_diag // DA_UNROLL, steps_per_trip(True), 0)
    lax.fori_loop(n_diag // DA_UNROLL, n_steps // DA_UNROLL, steps_per_trip(False), 0)
    stage_b(n_steps - 1, *slot[(n_steps - 1) % 2])

    lam_p = lam_ref[...]
    lam = (jnp.exp(jnp.sum(lam_p[0:1] * lam_p[1:2], axis=1, keepdims=True))
           - jnp.exp(jnp.sum(lam_p[2:3] * lam_p[3:4], axis=1, keepdims=True)) + lambda_init)
    g = g_ref[...]

    def finish(c, carry):
        rows = pl.ds(pl.multiple_of(c * t, t), t)
        a0 = acc[0, rows, :]
        a1 = acc[1, rows, :]
        o = a0[:, :LANES] / a0[:, LANES:] - lam * (a1[:, :LANES] / a1[:, LANES:])
        o_ref[0, rows, :] = (_rms(o, g, SUBLN_EPS) * (1.0 - lambda_init)).astype(o_ref.dtype)
        return carry

    lax.fori_loop(0, s_len // t, finish, 0)


def _da_attention(proj3, lam_params, g_subln, lambda_init):
    b, s, _ = proj3.shape
    t = ATTN_TILE
    n_tiles = s // t
    steps = _tile_schedule(n_tiles)
    assert n_tiles % DA_UNROLL == 0 and len(steps) % DA_UNROLL == 0
    qi_tab = jnp.asarray([q for q, _ in steps], jnp.int32)
    kj_tab = jnp.asarray([k for _, k in steps], jnp.int32)
    col_block = lambda off: pl.BlockSpec((1, s, LANES), lambda bi, h, *_: (bi, 0, off + h))
    return pl.pallas_call(
        functools.partial(_da_kernel, lambda_init=lambda_init, n_steps=len(steps), n_diag=n_tiles),
        grid_spec=pltpu.PrefetchScalarGridSpec(
            num_scalar_prefetch=2,
            grid=(b, DA_HEADS),
            in_specs=[
                pl.BlockSpec((4, HEAD_DIM), lambda bi, h, *_: (0, 0)),
                pl.BlockSpec((1, LANES), lambda bi, h, *_: (0, 0)),
                col_block(_QD), col_block(_KD), col_block(_VD),
            ],
            out_specs=pl.BlockSpec((1, s, LANES), lambda bi, h, *_: (bi, 0, h)),
            scratch_shapes=[
                pltpu.VMEM((2, s, LANES), BF16),
                pltpu.VMEM((s, 2 * LANES), BF16),
                pltpu.VMEM((2, s, 2 * LANES), F32),
                pltpu.VMEM((2, s, LANES), F32),
                pltpu.VMEM((2 * t, t), BF16), pltpu.VMEM((2 * t, t), BF16),
                pltpu.VMEM((2 * t, LANES), F32), pltpu.VMEM((2 * t, LANES), F32),
            ]),
        out_shape=jax.ShapeDtypeStruct((b, s, DA_V_WIDTH), BF16),
        compiler_params=pltpu.CompilerParams(
            dimension_semantics=("arbitrary", "arbitrary"), vmem_limit_bytes=VMEM_LIMIT),
        name="da_attn",
    )(qi_tab, kj_tab, lam_params, g_subln, proj3, proj3, proj3)


def _mix_ffn_kernel(x_ref, osb_ref, oda_ref, gsb_ref, gda_ref, wsb_ref, wda_ref, wout_ref,
                    g_ref, wg_ref, wu_ref, wd_ref, gf_ref, o_ref, *, final_norm):
    mixed = (gsb_ref[...].astype(F32) * _dot(osb_ref[...], wsb_ref[...])
             + gda_ref[...].astype(F32) * _dot(oda_ref[...], wda_ref[...]))
    x = x_ref[...] + _dot(mixed.astype(BF16), wout_ref[...])
    h = _rms(x, g_ref[...], NORM_EPS).astype(BF16)
    a = (jax.nn.silu(_dot(h, wg_ref[...])) * _dot(h, wu_ref[...])).astype(BF16)
    y = x + _dot(a, wd_ref[...])
    if final_norm:
        y = _rms(y, gf_ref[...], NORM_EPS)
    o_ref[...] = y


def _mix_ffn(x2, o_sb, o_da, proj, w_sb, w_da, w_out, g, w_gate, w_up, w_down, g_final, final_norm, tm):
    t = x2.shape[0]
    rows = lambda width, col=0: pl.BlockSpec((tm, width), lambda i: (i, col))
    whole = lambda shape: pl.BlockSpec(shape, lambda i: (0, 0), pipeline_mode=pl.Buffered(1))
    return pl.pallas_call(
        functools.partial(_mix_ffn_kernel, final_norm=final_norm),
        grid=(t // tm,),
        in_specs=[
            rows(D_MODEL), rows(SB_WIDTH), rows(DA_V_WIDTH), rows(D_MODEL, _GATE), rows(D_MODEL, _GATE + 1),
            whole((SB_WIDTH, D_MODEL)), whole((DA_V_WIDTH, D_MODEL)), whole((D_MODEL, D_MODEL)),
            whole((1, D_MODEL)), whole((D_MODEL, D_FF)), whole((D_MODEL, D_FF)), whole((D_FF, D_MODEL)),
            whole((1, D_MODEL)),
        ],
        out_specs=rows(D_MODEL),
        out_shape=jax.ShapeDtypeStruct((t, D_MODEL), F32),
        compiler_params=pltpu.CompilerParams(
            dimension_semantics=("arbitrary",), vmem_limit_bytes=VMEM_LIMIT),
        name="mix_ffn",
    )(x2, o_sb, o_da, proj, proj, w_sb, w_da, w_out, g, w_gate, w_up, w_down, g_final)


def _rope_tables(seq_len):
    inv_freq = 1.0 / (ROPE_THETA ** (jnp.arange(0, HEAD_DIM, 2, dtype=F32) / HEAD_DIM))
    ang = jnp.arange(seq_len, dtype=F32)[:, None] * inv_freq[None, :]
    ang = jnp.concatenate([ang, ang], axis=-1)
    sign = jnp.where(jnp.arange(HEAD_DIM) < HEAD_DIM // 2, -1.0, 1.0).astype(F32)
    cos = jnp.tile(jnp.cos(ang), (1, LANES // HEAD_DIM))
    sin = jnp.tile(jnp.sin(ang) * sign[None, :], (1, LANES // HEAD_DIM))
    return cos, sin


def kernel(x, g_mix, w_in, lambda_q1, lambda_k1, lambda_q2, lambda_k2, g_subln, w_branch_sb, w_branch_da,
           w_out, g_ffn, w_ffn_gate, w_ffn_up, w_ffn_down, g_final):
    b, s, d = x.shape
    depth = w_in.shape[0]
    assert d == D_MODEL and s % ATTN_TILE == 0
    tm = min(TOKEN_TILE, s)
    cos, sin = _rope_tables(s)
    x2 = x.reshape(b * s, d)
    for l in range(depth):
        lambda_init = 0.8 - 0.6 * math.exp(-0.3 * l)
        proj = _in_proj(x2, g_mix[l][None], w_in[l].astype(BF16), cos, sin, s, tm)
        proj3 = proj.reshape(b, s, IN_WIDTH)
        o_sb = _sb_attention(proj3)
        lam_params = jnp.stack([lambda_q1[l], lambda_k1[l], lambda_q2[l], lambda_k2[l]]).astype(F32)
        o_da = _da_attention(proj3, lam_params, g_subln[l][None], lambda_init)
        x2 = _mix_ffn(x2, o_sb.reshape(b * s, SB_WIDTH), o_da.reshape(b * s, DA_V_WIDTH), proj,
                      w_branch_sb[l].astype(BF16), w_branch_da[l].astype(BF16), w_out[l].astype(BF16),
                      g_ffn[l][None], w_ffn_gate[l].astype(BF16), w_ffn_up[l].astype(BF16),
                      w_ffn_down[l].astype(BF16), g_final[None], l == depth - 1, tm)
    return x2.reshape(b, s, d)
```

```python
import functools
import math

import jax
import jax.numpy as jnp
from jax import lax
from jax.experimental import pallas as pl
from jax.experimental.pallas import tpu as pltpu

D_MODEL = 1024
HEAD_DIM = 64
CHUNK = 64
SB_WIDTH = 512
DA_QK_WIDTH = 512
DA_V_WIDTH = 512
DA_HEADS = 4
N_BRANCH = 2
IN_WIDTH = 3 * SB_WIDTH + 2 * DA_QK_WIDTH + DA_V_WIDTH + N_BRANCH * D_MODEL
D_FF = 2816
ROPE_THETA = 10000.0
NORM_EPS = 1e-6
SUBLN_EPS = 1e-5
QK_SCALE = HEAD_DIM ** -0.5
LOG2E = math.log2(math.e)
EXP2_CLAMP = 126.0

LANES = 128
COL_TILE = 512
TOKEN_TILE = 512
ATTN_TILE = 256
SB_PHASE = 4
DA_PHASE = 8
VMEM_LIMIT = 56 * 1024 * 1024

_QA, _KA, _VA = 0, 4, 8
_QD, _KD, _VD = 12, 16, 20
_GATE = 3

F32 = jnp.float32
BF16 = jnp.bfloat16


def _rms(x, g, eps):
    return x * lax.rsqrt(jnp.mean(x * x, axis=-1, keepdims=True) + eps) * g


def _nt_dot(a, b):
    return lax.dot_general(a, b, (((1,), (1,)), ((), ())), preferred_element_type=F32)


def _dot(a, b):
    return jnp.dot(a, b, preferred_element_type=F32)


def _in_proj_kernel(x_ref, g_ref, w_ref, cos_ref, sin_ref, o_ref):
    h = _rms(x_ref[...], g_ref[...], NORM_EPS).astype(BF16)
    tm = h.shape[0]
    lane = lax.broadcasted_iota(jnp.int32, (tm, LANES), 1)
    first_half = (lane % HEAD_DIM) < (HEAD_DIM // 2)
    cos = cos_ref[...]
    sin = sin_ref[...]

    def rope(a):
        outs = []
        for c in range(a.shape[1] // LANES):
            xs = a[:, c * LANES:(c + 1) * LANES]
            rot = jnp.where(first_half, pltpu.roll(xs, LANES - HEAD_DIM // 2, 1),
                            pltpu.roll(xs, HEAD_DIM // 2, 1))
            outs.append(xs * cos + rot * sin)
        return jnp.concatenate(outs, axis=1)

    for j in range(IN_WIDTH // COL_TILE):
        cols = slice(j * COL_TILE, (j + 1) * COL_TILE)
        a = _dot(h, w_ref[:, cols])
        if j == 0:
            a = a * (QK_SCALE * LOG2E)
        elif j == 3:
            a = rope(a) * (QK_SCALE * LOG2E)
        elif j == 4:
            a = rope(a)
        elif j >= 6:
            a = jax.nn.sigmoid(a)
        o_ref[:, cols] = a.astype(o_ref.dtype)


def _in_proj(x2, g, w, cos, sin, seq, tm):
    t = x2.shape[0]
    pos_blocks = seq // tm
    return pl.pallas_call(
        _in_proj_kernel,
        grid=(t // tm,),
        in_specs=[
            pl.BlockSpec((tm, D_MODEL), lambda i: (i, 0)),
            pl.BlockSpec((1, D_MODEL), lambda i: (0, 0)),
            pl.BlockSpec((D_MODEL, IN_WIDTH), lambda i: (0, 0)),
            pl.BlockSpec((tm, LANES), lambda i: (i % pos_blocks, 0)),
            pl.BlockSpec((tm, LANES), lambda i: (i % pos_blocks, 0)),
        ],
        out_specs=pl.BlockSpec((tm, IN_WIDTH), lambda i: (i, 0)),
        out_shape=jax.ShapeDtypeStruct((t, IN_WIDTH), BF16),
        compiler_params=pltpu.CompilerParams(
            dimension_semantics=("arbitrary",), vmem_limit_bytes=VMEM_LIMIT),
        name="in_proj",
    )(x2, g, w, cos, sin)


def _tile_schedule(n_tiles):
    steps = [(i, i) for i in range(n_tiles)]
    steps += [(i, j) for i in range(n_tiles) for j in range(i - 1, -1, -1)]
    return steps


def _sb_kernel(qi_tab, kj_tab, q_ref, k_ref, v_ref, o_ref, qs, vm, u2, racc, oacc,
               z_sl, hl_sl, rs_sl, w_sl, sc_sl, *, n_steps, n_diag):
    t = ATTN_TILE
    s_len = q_ref.shape[1]
    lane = lax.broadcasted_iota(jnp.int32, (t, LANES), 1)
    head0 = lane < HEAD_DIM

    def prep(c, carry):
        rows = pl.ds(pl.multiple_of(c * t, t), t)
        qc = q_ref[0, rows, :]
        vc = v_ref[0, rows, :]
        zero = jnp.zeros_like(qc)
        qs[0, rows, :] = jnp.where(head0, qc, zero)
        qs[1, rows, :] = jnp.where(head0, zero, qc)
        vm[0, rows, :] = jnp.where(head0, vc, zero)
        vm[1, rows, :] = jnp.where(head0, zero, vc)
        oacc[rows, :] = jnp.zeros((t, LANES), F32)
        racc[0, rows, :] = jnp.zeros((t, LANES), F32)
        racc[1, rows, :] = jnp.zeros((t, LANES), F32)
        return carry

    lax.fori_loop(0, s_len // t, prep, 0)
    row = lax.broadcasted_iota(jnp.int32, (t, t), 0)
    col = lax.broadcasted_iota(jnp.int32, (t, t), 1)
    u2[...] = jnp.where(row > col, 1.0, 0.0).astype(BF16)

    def rows_of(tab, s):
        return pl.ds(pl.multiple_of(tab[s] * t, t), t)

    def stage_a(s, z_s, hl, rs_s, masked):
        qrows = rows_of(qi_tab, s)
        q2 = jnp.concatenate([qs[0, qrows, :], qs[1, qrows, :]], axis=0)
        z = _nt_dot(q2, k_ref[0, rows_of(kj_tab, s), :])
        sp = jnp.maximum(z, jnp.log2(1.0 + jnp.exp2(jnp.minimum(z, EXP2_CLAMP))))
        own = z - sp
        if masked:
            r2 = lax.broadcasted_iota(jnp.int32, (2 * t, t), 0)
            c2 = lax.broadcasted_iota(jnp.int32, (2 * t, t), 1)
            earlier = c2 < (r2 % t)
            sp = jnp.where(earlier, sp, 0.0)
            own = jnp.where(earlier, own, -jnp.inf)
        z_s[...] = own
        hl[...] = sp.astype(BF16)
        rs_s[...] = jnp.broadcast_to(jnp.sum(sp, axis=1, keepdims=True), (2 * t, LANES))

    def stage_b(s, z_s, hl, rs_s, w, sc):
        wv = jnp.exp2(z_s[...] - _dot(hl[...], u2[...])).astype(BF16)
        w[:, 0:t] = wv[:t]
        w[:, t:2 * t] = wv[t:]
        qrows = rows_of(qi_tab, s)
        rs = rs_s[...]
        r0 = racc[0, qrows, :]
        r1 = racc[1, qrows, :]
        racc[0, qrows, :] = r0 + rs[:t]
        racc[1, qrows, :] = r1 + rs[t:]
        sc[...] = jnp.exp2(-jnp.where(head0, r0, r1))

    def stage_c(s, w, sc):
        krows = rows_of(kj_tab, s)
        v2 = jnp.concatenate([vm[0, krows, :], vm[1, krows, :]], axis=0)
        oacc[rows_of(qi_tab, s), :] += _dot(w[...], v2) * sc[...]

    ph = SB_PHASE

    def phase(base, a_set, masked, do_a=True, do_b=True, do_c=True):
        b_set = 1 - a_set
        for u in range(ph):
            if do_a:
                stage_a(base + u, z_sl.at[a_set, u], hl_sl.at[a_set, u], rs_sl.at[a_set, u], masked)
            if do_b:
                stage_b(base - ph + u, z_sl.at[b_set, u], hl_sl.at[b_set, u], rs_sl.at[b_set, u],
                        w_sl.at[b_set, u], sc_sl.at[b_set, u])
            if do_c:
                stage_c(base - 2 * ph + u, w_sl.at[a_set, u], sc_sl.at[a_set, u])

    def four_phases(j, carry):
        for q4 in range(4):
            phase(4 * ph * j + q4 * ph, q4 % 2, False)
        return carry

    n_ph = n_steps // ph
    assert n_diag == 4 * ph and (n_ph - 4) % 4 == 2
    phase(0, 0, True, do_b=False, do_c=False)
    phase(ph, 1, True, do_c=False)
    phase(2 * ph, 0, True)
    phase(3 * ph, 1, True)
    lax.fori_loop(1, n_ph // 4, four_phases, 0)
    phase(n_steps - 2 * ph, 0, False)
    phase(n_steps - ph, 1, False)
    phase(n_steps, 0, False, do_a=False)
    phase(n_steps + ph, 1, False, do_a=False, do_b=False)
    o_ref[0] = oacc[...].astype(o_ref.dtype)


def _sb_attention(proj3):
    b, s, _ = proj3.shape
    t = ATTN_TILE
    n_tiles = s // t
    steps = _tile_schedule(n_tiles)
    assert n_tiles % (2 * SB_PHASE) == 0 and len(steps) % (2 * SB_PHASE) == 0
    qi_tab = jnp.asarray([q for q, _ in steps], jnp.int32)
    kj_tab = jnp.asarray([k for _, k in steps], jnp.int32)
    col_block = lambda off: pl.BlockSpec((1, s, LANES), lambda bi, p, *_: (bi, 0, off + p))
    return pl.pallas_call(
        functools.partial(_sb_kernel, n_steps=len(steps), n_diag=n_tiles),
        grid_spec=pltpu.PrefetchScalarGridSpec(
            num_scalar_prefetch=2,
            grid=(b, SB_WIDTH // LANES),
            in_specs=[col_block(_QA), col_block(_KA), col_block(_VA)],
            out_specs=pl.BlockSpec((1, s, LANES), lambda bi, p, *_: (bi, 0, p)),
            scratch_shapes=[
                pltpu.VMEM((2, s, LANES), BF16),
                pltpu.VMEM((2, s, LANES), BF16),
                pltpu.VMEM((t, t), BF16),
                pltpu.VMEM((2, s, LANES), F32),
                pltpu.VMEM((s, LANES), F32),
                pltpu.VMEM((2, SB_PHASE, 2 * t, t), F32),
                pltpu.VMEM((2, SB_PHASE, 2 * t, t), BF16),
                pltpu.VMEM((2, SB_PHASE, 2 * t, LANES), F32),
                pltpu.VMEM((2, SB_PHASE, t, 2 * t), BF16),
                pltpu.VMEM((2, SB_PHASE, t, LANES), F32),
            ]),
        out_shape=jax.ShapeDtypeStruct((b, s, SB_WIDTH), BF16),
        compiler_params=pltpu.CompilerParams(
            dimension_semantics=("arbitrary", "arbitrary"), vmem_limit_bytes=VMEM_LIMIT),
        name="sb_attn",
    )(qi_tab, kj_tab, proj3, proj3, proj3)


def _da_kernel(qi_tab, kj_tab, lam_ref, g_ref, q_ref, k_ref, v_ref, o_ref, qs, vx, acc, macc,
               p_sl, al_sl, *, lambda_init, n_steps, n_diag):
    t = ATTN_TILE
    s_len = q_ref.shape[1]
    lane = lax.broadcasted_iota(jnp.int32, (t, LANES), 1)
    map0 = lane < HEAD_DIM

    def prep(c, carry):
        rows = pl.ds(pl.multiple_of(c * t, t), t)
        qc = q_ref[0, rows, :]
        zero = jnp.zeros_like(qc)
        qs[0, rows, :] = jnp.where(map0, qc, zero)
        qs[1, rows, :] = jnp.where(map0, zero, qc)
        vx[rows, 0:LANES] = v_ref[0, rows, :]
        vx[rows, LANES:2 * LANES] = jnp.ones((t, LANES), BF16)
        for c2 in range(2):
            acc[c2, rows, :] = jnp.zeros((t, 2 * LANES), F32)
            macc[c2, rows, :] = jnp.full((t, LANES), -jnp.inf, F32)
        return carry

    lax.fori_loop(0, s_len // t, prep, 0)

    def rows_of(tab, s):
        return pl.ds(pl.multiple_of(tab[s] * t, t), t)

    def stage_a(s, p_s, al_s, masked):
        qrows = rows_of(qi_tab, s)
        q2 = jnp.concatenate([qs[0, qrows, :], qs[1, qrows, :]], axis=0)
        sc = _nt_dot(q2, k_ref[0, rows_of(kj_tab, s), :])
        if masked:
            r2 = lax.broadcasted_iota(jnp.int32, (2 * t, t), 0)
            c2 = lax.broadcasted_iota(jnp.int32, (2 * t, t), 1)
            sc = jnp.where((c2 // CHUNK) <= ((r2 % t) // CHUNK), sc, -jnp.inf)
        m_prev = jnp.concatenate([macc[0, qrows, :], macc[1, qrows, :]], axis=0)
        m_new = jnp.maximum(m_prev, jnp.max(sc, axis=1, keepdims=True))
        p_s[...] = jnp.exp2(sc - jnp.concatenate([m_new] * (t // LANES), axis=1)).astype(BF16)
        al_s[...] = jnp.exp2(m_prev - m_new)
        for c in range(2):
            macc[c, qrows, :] = m_new[c * t:(c + 1) * t]

    def stage_b(s, p_s, al_s):
        qrows = rows_of(qi_tab, s)
        pv = _dot(p_s[...], vx[rows_of(kj_tab, s), :])
        al = al_s[...]
        al = jnp.concatenate([al, al], axis=1)
        for c in range(2):
            acc[c, qrows, :] = al[c * t:(c + 1) * t] * acc[c, qrows, :] + pv[c * t:(c + 1) * t]

    ph = DA_PHASE

    def phase(base, a_set, masked, do_a=True, do_b=True):
        b_set = 1 - a_set
        for u in range(ph):
            if do_a:
                stage_a(base + u, p_sl.at[a_set, u], al_sl.at[a_set, u], masked)
            if do_b:
                stage_b(base - ph + u, p_sl.at[b_set, u], al_sl.at[b_set, u])

    def two_phases(masked):
        def body(j, carry):
            phase(2 * ph * j, 0, masked)
            phase(2 * ph * j + ph, 1, masked)
            return carry
        return body

    assert n_diag == 2 * ph and (n_steps - n_diag) % (2 * ph) == ph
    phase(0, 0, True, do_b=False)
    phase(ph, 1, True)
    lax.fori_loop(1, n_steps // (2 * ph), two_phases(False), 0)
    phase(n_steps - ph, 0, False)
    phase(n_steps, 1, False, do_a=False)

    lam_p = lam_ref[...]
    lam = (jnp.exp(jnp.sum(lam_p[0:1] * lam_p[1:2], axis=1, keepdims=True))
           - jnp.exp(jnp.sum(lam_p[2:3] * lam_p[3:4], axis=1, keepdims=True)) + lambda_init)
    g = g_ref[...]

    def finish(c, carry):
        rows = pl.ds(pl.multiple_of(c * t, t), t)
        a0 = acc[0, rows, :]
        a1 = acc[1, rows, :]
        o = a0[:, :LANES] / a0[:, LANES:] - lam * (a1[:, :LANES] / a1[:, LANES:])
        o_ref[0, rows, :] = (_rms(o, g, SUBLN_EPS) * (1.0 - lambda_init)).astype(o_ref.dtype)
        return carry

    lax.fori_loop(0, s_len // t, finish, 0)


def _da_attention(proj3, lam_params, g_subln, lambda_init):
    b, s, _ = proj3.shape
    t = ATTN_TILE
    n_tiles = s // t
    steps = _tile_schedule(n_tiles)
    assert n_tiles == 2 * DA_PHASE
    qi_tab = jnp.asarray([q for q, _ in steps], jnp.int32)
    kj_tab = jnp.asarray([k for _, k in steps], jnp.int32)
    col_block = lambda off: pl.BlockSpec((1, s, LANES), lambda bi, h, *_: (bi, 0, off + h))
    return pl.pallas_call(
        functools.partial(_da_kernel, lambda_init=lambda_init, n_steps=len(steps), n_diag=n_tiles),
        grid_spec=pltpu.PrefetchScalarGridSpec(
            num_scalar_prefetch=2,
            grid=(b, DA_HEADS),
            in_specs=[
                pl.BlockSpec((4, HEAD_DIM), lambda bi, h, *_: (0, 0)),
                pl.BlockSpec((1, LANES), lambda bi, h, *_: (0, 0)),
                col_block(_QD), col_block(_KD), col_block(_VD),
            ],
            out_specs=pl.BlockSpec((1, s, LANES), lambda bi, h, *_: (bi, 0, h)),
            scratch_shapes=[
                pltpu.VMEM((2, s, LANES), BF16),
                pltpu.VMEM((s, 2 * LANES), BF16),
                pltpu.VMEM((2, s, 2 * LANES), F32),
                pltpu.VMEM((2, s, LANES), F32),
                pltpu.VMEM((2, DA_PHASE, 2 * t, t), BF16),
                pltpu.VMEM((2, DA_PHASE, 2 * t, LANES), F32),
            ]),
        out_shape=jax.ShapeDtypeStruct((b, s, DA_V_WIDTH), BF16),
        compiler_params=pltpu.CompilerParams(
            dimension_semantics=("arbitrary", "arbitrary"), vmem_limit_bytes=VMEM_LIMIT),
        name="da_attn",
    )(qi_tab, kj_tab, lam_params, g_subln, proj3, proj3, proj3)


def _mix_ffn_kernel(x_ref, osb_ref, oda_ref, gsb_ref, gda_ref, wsb_ref, wda_ref, wout_ref,
                    g_ref, wg_ref, wu_ref, wd_ref, gf_ref, o_ref, *, final_norm):
    mixed = (gsb_ref[...].astype(F32) * _dot(osb_ref[...], wsb_ref[...])
             + gda_ref[...].astype(F32) * _dot(oda_ref[...], wda_ref[...]))
    x = x_ref[...] + _dot(mixed.astype(BF16), wout_ref[...])
    h = _rms(x, g_ref[...], NORM_EPS).astype(BF16)
    a = (jax.nn.silu(_dot(h, wg_ref[...])) * _dot(h, wu_ref[...])).astype(BF16)
    y = x + _dot(a, wd_ref[...])
    if final_norm:
        y = _rms(y, gf_ref[...], NORM_EPS)
    o_ref[...] = y


def _mix_ffn(x2, o_sb, o_da, proj, w_sb, w_da, w_out, g, w_gate, w_up, w_down, g_final, final_norm, tm):
    t = x2.shape[0]
    rows = lambda width, col=0: pl.BlockSpec((tm, width), lambda i: (i, col))
    whole = lambda shape: pl.BlockSpec(shape, lambda i: (0, 0), pipeline_mode=pl.Buffered(1))
    return pl.pallas_call(
        functools.partial(_mix_ffn_kernel, final_norm=final_norm),
        grid=(t // tm,),
        in_specs=[
            rows(D_MODEL), rows(SB_WIDTH), rows(DA_V_WIDTH), rows(D_MODEL, _GATE), rows(D_MODEL, _GATE + 1),
            whole((SB_WIDTH, D_MODEL)), whole((DA_V_WIDTH, D_MODEL)), whole((D_MODEL, D_MODEL)),
            whole((1, D_MODEL)), whole((D_MODEL, D_FF)), whole((D_MODEL, D_FF)), whole((D_FF, D_MODEL)),
            whole((1, D_MODEL)),
        ],
        out_specs=rows(D_MODEL),
        out_shape=jax.ShapeDtypeStruct((t, D_MODEL), F32),
        compiler_params=pltpu.CompilerParams(
            dimension_semantics=("arbitrary",), vmem_limit_bytes=VMEM_LIMIT),
        name="mix_ffn",
    )(x2, o_sb, o_da, proj, proj, w_sb, w_da, w_out, g, w_gate, w_up, w_down, g_final)


def _rope_tables(seq_len):
    inv_freq = 1.0 / (ROPE_THETA ** (jnp.arange(0, HEAD_DIM, 2, dtype=F32) / HEAD_DIM))
    ang = jnp.arange(seq_len, dtype=F32)[:, None] * inv_freq[None, :]
    ang = jnp.concatenate([ang, ang], axis=-1)
    sign = jnp.where(jnp.arange(HEAD_DIM) < HEAD_DIM // 2, -1.0, 1.0).astype(F32)
    cos = jnp.tile(jnp.cos(ang), (1, LANES // HEAD_DIM))
    sin = jnp.tile(jnp.sin(ang) * sign[None, :], (1, LANES // HEAD_DIM))
    return cos, sin


def kernel(x, g_mix, w_in, lambda_q1, lambda_k1, lambda_q2, lambda_k2, g_subln, w_branch_sb, w_branch_da,
           w_out, g_ffn, w_ffn_gate, w_ffn_up, w_ffn_down, g_final):
    b, s, d = x.shape
    depth = w_in.shape[0]
    assert d == D_MODEL and s % ATTN_TILE == 0
    tm = min(TOKEN_TILE, s)
    cos, sin = _rope_tables(s)
    x2 = x.reshape(b * s, d)
    for l in range(depth):
        lambda_init = 0.8 - 0.6 * math.exp(-0.3 * l)
        proj = _in_proj(x2, g_mix[l][None], w_in[l].astype(BF16), cos, sin, s, tm)
        proj3 = proj.reshape(b, s, IN_WIDTH)
        o_sb = _sb_attention(proj3)
        lam_params = jnp.stack([lambda_q1[l], lambda_k1[l], lambda_q2[l], lambda_k2[l]]).astype(F32)
        o_da = _da_attention(proj3, lam_params, g_subln[l][None], lambda_init)
        x2 = _mix_ffn(x2, o_sb.reshape(b * s, SB_WIDTH), o_da.reshape(b * s, DA_V_WIDTH), proj,
                      w_branch_sb[l].astype(BF16), w_branch_da[l].astype(BF16), w_out[l].astype(BF16),
                      g_ffn[l][None], w_ffn_gate[l].astype(BF16), w_ffn_up[l].astype(BF16),
                      w_ffn_down[l].astype(BF16), g_final[None], l == depth - 1, tm)
    return x2.reshape(b, s, d)
```

```python
import functools
import math

import jax
import jax.numpy as jnp
from jax import lax
from jax.experimental import pallas as pl
from jax.experimental.pallas import tpu as pltpu

D_MODEL = 1024
HEAD_DIM = 64
CHUNK = 64
SB_WIDTH = 512
DA_QK_WIDTH = 512
DA_V_WIDTH = 512
DA_HEADS = 4
N_BRANCH = 2
IN_WIDTH = 3 * SB_WIDTH + 2 * DA_QK_WIDTH + DA_V_WIDTH + N_BRANCH * D_MODEL
D_FF = 2816
ROPE_THETA = 10000.0
NORM_EPS = 1e-6
SUBLN_EPS = 1e-5
QK_SCALE = HEAD_DIM ** -0.5
LOG2E = math.log2(math.e)
EXP2_CLAMP = 126.0

LANES = 128
COL_TILE = 512
TOKEN_TILE = 512
ATTN_TILE = 256
SB_PHASE = 4
DA_PHASE = 4
TRIP_PHASES = 6
VMEM_LIMIT = 56 * 1024 * 1024

_QA, _KA, _VA = 0, 4, 8
_QD, _KD, _VD = 12, 16, 20
_GATE = 3

F32 = jnp.float32
BF16 = jnp.bfloat16


def _rms(x, g, eps):
    return x * lax.rsqrt(jnp.mean(x * x, axis=-1, keepdims=True) + eps) * g


def _nt_dot(a, b):
    return lax.dot_general(a, b, (((1,), (1,)), ((), ())), preferred_element_type=F32)


def _dot(a, b):
    return jnp.dot(a, b, preferred_element_type=F32)


def _in_proj_kernel(x_ref, g_ref, w_ref, cos_ref, sin_ref, o_ref):
    h = _rms(x_ref[...], g_ref[...], NORM_EPS).astype(BF16)
    tm = h.shape[0]
    lane = lax.broadcasted_iota(jnp.int32, (tm, LANES), 1)
    first_half = (lane % HEAD_DIM) < (HEAD_DIM // 2)
    cos = cos_ref[...]
    sin = sin_ref[...]

    def rope(a):
        outs = []
        for c in range(a.shape[1] // LANES):
            xs = a[:, c * LANES:(c + 1) * LANES]
            rot = jnp.where(first_half, pltpu.roll(xs, LANES - HEAD_DIM // 2, 1),
                            pltpu.roll(xs, HEAD_DIM // 2, 1))
            outs.append(xs * cos + rot * sin)
        return jnp.concatenate(outs, axis=1)

    for j in range(IN_WIDTH // COL_TILE):
        cols = slice(j * COL_TILE, (j + 1) * COL_TILE)
        a = _dot(h, w_ref[:, cols])
        if j == 0:
            a = a * (QK_SCALE * LOG2E)
        elif j == 3:
            a = rope(a) * (QK_SCALE * LOG2E)
        elif j == 4:
            a = rope(a)
        elif j >= 6:
            a = jax.nn.sigmoid(a)
        o_ref[:, cols] = a.astype(o_ref.dtype)


def _in_proj(x2, g, w, cos, sin, seq, tm):
    t = x2.shape[0]
    pos_blocks = seq // tm
    return pl.pallas_call(
        _in_proj_kernel,
        grid=(t // tm,),
        in_specs=[
            pl.BlockSpec((tm, D_MODEL), lambda i: (i, 0)),
            pl.BlockSpec((1, D_MODEL), lambda i: (0, 0)),
            pl.BlockSpec((D_MODEL, IN_WIDTH), lambda i: (0, 0)),
            pl.BlockSpec((tm, LANES), lambda i: (i % pos_blocks, 0)),
            pl.BlockSpec((tm, LANES), lambda i: (i % pos_blocks, 0)),
        ],
        out_specs=pl.BlockSpec((tm, IN_WIDTH), lambda i: (i, 0)),
        out_shape=jax.ShapeDtypeStruct((t, IN_WIDTH), BF16),
        compiler_params=pltpu.CompilerParams(
            dimension_semantics=("arbitrary",), vmem_limit_bytes=VMEM_LIMIT),
        name="in_proj",
    )(x2, g, w, cos, sin)


def _tile_schedule(n_tiles):
    steps = [(i, i) for i in range(n_tiles)]
    steps += [(i, j) for i in range(n_tiles) for j in range(i - 1, -1, -1)]
    return steps


def _sb_kernel(qi_tab, kj_tab, q_ref, k_ref, v_ref, o_ref, qs, vm, u2, racc, oacc,
               z_sl, hl_sl, rs_sl, w_sl, sc_sl, *, n_steps, n_diag):
    t = ATTN_TILE
    s_len = q_ref.shape[1]
    lane = lax.broadcasted_iota(jnp.int32, (t, LANES), 1)
    head0 = lane < HEAD_DIM

    def prep(c, carry):
        rows = pl.ds(pl.multiple_of(c * t, t), t)
        qc = q_ref[0, rows, :]
        vc = v_ref[0, rows, :]
        zero = jnp.zeros_like(qc)
        qs[0, rows, :] = jnp.where(head0, qc, zero)
        qs[1, rows, :] = jnp.where(head0, zero, qc)
        vm[0, rows, :] = jnp.where(head0, vc, zero)
        vm[1, rows, :] = jnp.where(head0, zero, vc)
        oacc[rows, :] = jnp.zeros((t, LANES), F32)
        racc[0, rows, :] = jnp.zeros((t, LANES), F32)
        racc[1, rows, :] = jnp.zeros((t, LANES), F32)
        return carry

    lax.fori_loop(0, s_len // t, prep, 0)
    row = lax.broadcasted_iota(jnp.int32, (t, t), 0)
    col = lax.broadcasted_iota(jnp.int32, (t, t), 1)
    u2[...] = jnp.where(row > col, 1.0, 0.0).astype(BF16)

    def rows_of(tab, s):
        return pl.ds(pl.multiple_of(tab[s] * t, t), t)

    def stage_a(s, z_s, hl, rs_s, masked):
        qrows = rows_of(qi_tab, s)
        q2 = jnp.concatenate([qs[0, qrows, :], qs[1, qrows, :]], axis=0)
        z = _nt_dot(q2, k_ref[0, rows_of(kj_tab, s), :])
        sp = jnp.maximum(z, jnp.log2(1.0 + jnp.exp2(jnp.minimum(z, EXP2_CLAMP))))
        own = z - sp
        if masked:
            r2 = lax.broadcasted_iota(jnp.int32, (2 * t, t), 0)
            c2 = lax.broadcasted_iota(jnp.int32, (2 * t, t), 1)
            earlier = c2 < (r2 % t)
            sp = jnp.where(earlier, sp, 0.0)
            own = jnp.where(earlier, own, -jnp.inf)
        z_s[...] = own
        hl[...] = sp.astype(BF16)
        rs_s[...] = jnp.broadcast_to(jnp.sum(sp, axis=1, keepdims=True), (2 * t, LANES))

    def stage_b(s, z_s, hl, rs_s, w, sc):
        wv = jnp.exp2(z_s[...] - _dot(hl[...], u2[...])).astype(BF16)
        w[:, 0:t] = wv[:t]
        w[:, t:2 * t] = wv[t:]
        qrows = rows_of(qi_tab, s)
        rs = rs_s[...]
        r0 = racc[0, qrows, :]
        r1 = racc[1, qrows, :]
        racc[0, qrows, :] = r0 + rs[:t]
        racc[1, qrows, :] = r1 + rs[t:]
        sc[...] = jnp.exp2(-jnp.where(head0, r0, r1))

    def stage_c(s, w, sc):
        krows = rows_of(kj_tab, s)
        v2 = jnp.concatenate([vm[0, krows, :], vm[1, krows, :]], axis=0)
        oacc[rows_of(qi_tab, s), :] += _dot(w[...], v2) * sc[...]

    ph = SB_PHASE

    def phase(base, a_set, masked, do_a=True, do_b=True, do_c=True):
        b_set = 1 - a_set
        for u in range(ph):
            if do_a:
                stage_a(base + u, z_sl.at[a_set, u], hl_sl.at[a_set, u], rs_sl.at[a_set, u], masked)
            if do_b:
                stage_b(base - ph + u, z_sl.at[b_set, u], hl_sl.at[b_set, u], rs_sl.at[b_set, u],
                        w_sl.at[b_set, u], sc_sl.at[b_set, u])
            if do_c:
                stage_c(base - 2 * ph + u, w_sl.at[a_set, u], sc_sl.at[a_set, u])

    def trip(j, carry):
        for q in range(TRIP_PHASES):
            phase(n_diag + (TRIP_PHASES * j + q) * ph, q % 2, False)
        return carry

    n_ph = n_steps // ph
    assert n_diag == 4 * ph and (n_ph - 4) % TRIP_PHASES == 0 and TRIP_PHASES % 2 == 0
    phase(0, 0, True, do_b=False, do_c=False)
    phase(ph, 1, True, do_c=False)
    phase(2 * ph, 0, True)
    phase(3 * ph, 1, True)
    lax.fori_loop(0, (n_ph - 4) // TRIP_PHASES, trip, 0)
    phase(n_steps, 0, False, do_a=False)
    phase(n_steps + ph, 1, False, do_a=False, do_b=False)
    o_ref[0] = oacc[...].astype(o_ref.dtype)


def _sb_attention(proj3):
    b, s, _ = proj3.shape
    t = ATTN_TILE
    n_tiles = s // t
    steps = _tile_schedule(n_tiles)
    assert n_tiles % (2 * SB_PHASE) == 0 and len(steps) % (2 * SB_PHASE) == 0
    qi_tab = jnp.asarray([q for q, _ in steps], jnp.int32)
    kj_tab = jnp.asarray([k for _, k in steps], jnp.int32)
    col_block = lambda off: pl.BlockSpec((1, s, LANES), lambda bi, p, *_: (bi, 0, off + p))
    return pl.pallas_call(
        functools.partial(_sb_kernel, n_steps=len(steps), n_diag=n_tiles),
        grid_spec=pltpu.PrefetchScalarGridSpec(
            num_scalar_prefetch=2,
            grid=(b, SB_WIDTH // LANES),
            in_specs=[col_block(_QA), col_block(_KA), col_block(_VA)],
            out_specs=pl.BlockSpec((1, s, LANES), lambda bi, p, *_: (bi, 0, p)),
            scratch_shapes=[
                pltpu.VMEM((2, s, LANES), BF16),
                pltpu.VMEM((2, s, LANES), BF16),
                pltpu.VMEM((t, t), BF16),
                pltpu.VMEM((2, s, LANES), F32),
                pltpu.VMEM((s, LANES), F32),
                pltpu.VMEM((2, SB_PHASE, 2 * t, t), F32),
                pltpu.VMEM((2, SB_PHASE, 2 * t, t), BF16),
                pltpu.VMEM((2, SB_PHASE, 2 * t, LANES), F32),
                pltpu.VMEM((2, SB_PHASE, t, 2 * t), BF16),
                pltpu.VMEM((2, SB_PHASE, t, LANES), F32),
            ]),
        out_shape=jax.ShapeDtypeStruct((b, s, SB_WIDTH), BF16),
        compiler_params=pltpu.CompilerParams(
            dimension_semantics=("arbitrary", "arbitrary"), vmem_limit_bytes=VMEM_LIMIT),
        name="sb_attn",
    )(qi_tab, kj_tab, proj3, proj3, proj3)


def _da_kernel(qi_tab, kj_tab, lam_ref, g_ref, q_ref, k_ref, v_ref, o_ref, qs, vx, acc, macc,
               p_sl, al_sl, *, lambda_init, n_steps, n_diag):
    t = ATTN_TILE
    s_len = q_ref.shape[1]
    lane = lax.broadcasted_iota(jnp.int32, (t, LANES), 1)
    map0 = lane < HEAD_DIM

    def prep(c, carry):
        rows = pl.ds(pl.multiple_of(c * t, t), t)
        qc = q_ref[0, rows, :]
        zero = jnp.zeros_like(qc)
        qs[0, rows, :] = jnp.where(map0, qc, zero)
        qs[1, rows, :] = jnp.where(map0, zero, qc)
        vx[rows, 0:LANES] = v_ref[0, rows, :]
        vx[rows, LANES:2 * LANES] = jnp.ones((t, LANES), BF16)
        for c2 in range(2):
            acc[c2, rows, :] = jnp.zeros((t, 2 * LANES), F32)
            macc[c2, rows, :] = jnp.full((t, LANES), -jnp.inf, F32)
        return carry

    lax.fori_loop(0, s_len // t, prep, 0)

    def rows_of(tab, s):
        return pl.ds(pl.multiple_of(tab[s] * t, t), t)

    def stage_a(s, p_s, al_s, masked):
        qrows = rows_of(qi_tab, s)
        q2 = jnp.concatenate([qs[0, qrows, :], qs[1, qrows, :]], axis=0)
        sc = _nt_dot(q2, k_ref[0, rows_of(kj_tab, s), :])
        if masked:
            r2 = lax.broadcasted_iota(jnp.int32, (2 * t, t), 0)
            c2 = lax.broadcasted_iota(jnp.int32, (2 * t, t), 1)
            sc = jnp.where((c2 // CHUNK) <= ((r2 % t) // CHUNK), sc, -jnp.inf)
        m_prev = jnp.concatenate([macc[0, qrows, :], macc[1, qrows, :]], axis=0)
        m_new = jnp.maximum(m_prev, jnp.max(sc, axis=1, keepdims=True))
        p_s[...] = jnp.exp2(sc - jnp.concatenate([m_new] * (t // LANES), axis=1)).astype(BF16)
        al_s[...] = jnp.exp2(m_prev - m_new)
        for c in range(2):
            macc[c, qrows, :] = m_new[c * t:(c + 1) * t]

    def stage_b(s, p_s, al_s):
        qrows = rows_of(qi_tab, s)
        pv = _dot(p_s[...], vx[rows_of(kj_tab, s), :])
        al = al_s[...]
        al = jnp.concatenate([al, al], axis=1)
        for c in range(2):
            acc[c, qrows, :] = al[c * t:(c + 1) * t] * acc[c, qrows, :] + pv[c * t:(c + 1) * t]

    ph = DA_PHASE

    def phase(base, a_set, masked, do_a=True, do_b=True):
        b_set = 1 - a_set
        for u in range(ph):
            if do_a:
                stage_a(base + u, p_sl.at[a_set, u], al_sl.at[a_set, u], masked)
            if do_b:
                stage_b(base - ph + u, p_sl.at[b_set, u], al_sl.at[b_set, u])

    def trip(j, carry):
        for q in range(TRIP_PHASES):
            phase(n_diag + (TRIP_PHASES * j + q) * ph, q % 2, False)
        return carry

    n_ph = n_steps // ph
    assert n_diag == 4 * ph and (n_ph - 4) % TRIP_PHASES == 0 and TRIP_PHASES % 2 == 0
    phase(0, 0, True, do_b=False)
    phase(ph, 1, True)
    phase(2 * ph, 0, True)
    phase(3 * ph, 1, True)
    lax.fori_loop(0, (n_ph - 4) // TRIP_PHASES, trip, 0)
    phase(n_steps, 0, False, do_a=False)

    lam_p = lam_ref[...]
    lam = (jnp.exp(jnp.sum(lam_p[0:1] * lam_p[1:2], axis=1, keepdims=True))
           - jnp.exp(jnp.sum(lam_p[2:3] * lam_p[3:4], axis=1, keepdims=True)) + lambda_init)
    g = g_ref[...]

    def finish(c, carry):
        rows = pl.ds(pl.multiple_of(c * t, t), t)
        a0 = acc[0, rows, :]
        a1 = acc[1, rows, :]
        o = a0[:, :LANES] / a0[:, LANES:] - lam * (a1[:, :LANES] / a1[:, LANES:])
        o_ref[0, rows, :] = (_rms(o, g, SUBLN_EPS) * (1.0 - lambda_init)).astype(o_ref.dtype)
        return carry

    lax.fori_loop(0, s_len // t, finish, 0)


def _da_attention(proj3, lam_params, g_subln, lambda_init):
    b, s, _ = proj3.shape
    t = ATTN_TILE
    n_tiles = s // t
    steps = _tile_schedule(n_tiles)
    assert n_tiles == 4 * DA_PHASE
    qi_tab = jnp.asarray([q for q, _ in steps], jnp.int32)
    kj_tab = jnp.asarray([k for _, k in steps], jnp.int32)
    col_block = lambda off: pl.BlockSpec((1, s, LANES), lambda bi, h, *_: (bi, 0, off + h))
    return pl.pallas_call(
        functools.partial(_da_kernel, lambda_init=lambda_init, n_steps=len(steps), n_diag=n_tiles),
        grid_spec=pltpu.PrefetchScalarGridSpec(
            num_scalar_prefetch=2,
            grid=(b, DA_HEADS),
            in_specs=[
                pl.BlockSpec((4, HEAD_DIM), lambda bi, h, *_: (0, 0)),
                pl.BlockSpec((1, LANES), lambda bi, h, *_: (0, 0)),
                col_block(_QD), col_block(_KD), col_block(_VD),
            ],
            out_specs=pl.BlockSpec((1, s, LANES), lambda bi, h, *_: (bi, 0, h)),
            scratch_shapes=[
                pltpu.VMEM((2, s, LANES), BF16),
                pltpu.VMEM((s, 2 * LANES), BF16),
                pltpu.VMEM((2, s, 2 * LANES), F32),
                pltpu.VMEM((2, s, LANES), F32),
                pltpu.VMEM((2, DA_PHASE, 2 * t, t), BF16),
                pltpu.VMEM((2, DA_PHASE, 2 * t, LANES), F32),
            ]),
        out_shape=jax.ShapeDtypeStruct((b, s, DA_V_WIDTH), BF16),
        compiler_params=pltpu.CompilerParams(
            dimension_semantics=("arbitrary", "arbitrary"), vmem_limit_bytes=VMEM_LIMIT),
        name="da_attn",
    )(qi_tab, kj_tab, lam_params, g_subln, proj3, proj3, proj3)


def _mix_ffn_kernel(x_ref, osb_ref, oda_ref, gsb_ref, gda_ref, wsb_ref, wda_ref, wout_ref,
                    g_ref, wg_ref, wu_ref, wd_ref, gf_ref, o_ref, *, final_norm):
    mixed = (gsb_ref[...].astype(F32) * _dot(osb_ref[...], wsb_ref[...])
             + gda_ref[...].astype(F32) * _dot(oda_ref[...], wda_ref[...]))
    x = x_ref[...] + _dot(mixed.astype(BF16), wout_ref[...])
    h = _rms(x, g_ref[...], NORM_EPS).astype(BF16)
    a = (jax.nn.silu(_dot(h, wg_ref[...])) * _dot(h, wu_ref[...])).astype(BF16)
    y = x + _dot(a, wd_ref[...])
    if final_norm:
        y = _rms(y, gf_ref[...], NORM_EPS)
    o_ref[...] = y


def _mix_ffn(x2, o_sb, o_da, proj, w_sb, w_da, w_out, g, w_gate, w_up, w_down, g_final, final_norm, tm):
    t = x2.shape[0]
    rows = lambda width, col=0: pl.BlockSpec((tm, width), lambda i: (i, col))
    whole = lambda shape: pl.BlockSpec(shape, lambda i: (0, 0), pipeline_mode=pl.Buffered(1))
    return pl.pallas_call(
        functools.partial(_mix_ffn_kernel, final_norm=final_norm),
        grid=(t // tm,),
        in_specs=[
            rows(D_MODEL), rows(SB_WIDTH), rows(DA_V_WIDTH), rows(D_MODEL, _GATE), rows(D_MODEL, _GATE + 1),
            whole((SB_WIDTH, D_MODEL)), whole((DA_V_WIDTH, D_MODEL)), whole((D_MODEL, D_MODEL)),
            whole((1, D_MODEL)), whole((D_MODEL, D_FF)), whole((D_MODEL, D_FF)), whole((D_FF, D_MODEL)),
            whole((1, D_MODEL)),
        ],
        out_specs=rows(D_MODEL),
        out_shape=jax.ShapeDtypeStruct((t, D_MODEL), F32),
        compiler_params=pltpu.CompilerParams(
            dimension_semantics=("arbitrary",), vmem_limit_bytes=VMEM_LIMIT),
        name="mix_ffn",
    )(x2, o_sb, o_da, proj, proj, w_sb, w_da, w_out, g, w_gate, w_up, w_down, g_final)


def _rope_tables(seq_len):
    inv_freq = 1.0 / (ROPE_THETA ** (jnp.arange(0, HEAD_DIM, 2, dtype=F32) / HEAD_DIM))
    ang = jnp.arange(seq_len, dtype=F32)[:, None] * inv_freq[None, :]
    ang = jnp.concatenate([ang, ang], axis=-1)
    sign = jnp.where(jnp.arange(HEAD_DIM) < HEAD_DIM // 2, -1.0, 1.0).astype(F32)
    cos = jnp.tile(jnp.cos(ang), (1, LANES // HEAD_DIM))
    sin = jnp.tile(jnp.sin(ang) * sign[None, :], (1, LANES // HEAD_DIM))
    return cos, sin


def kernel(x, g_mix, w_in, lambda_q1, lambda_k1, lambda_q2, lambda_k2, g_subln, w_branch_sb, w_branch_da,
           w_out, g_ffn, w_ffn_gate, w_ffn_up, w_ffn_down, g_final):
    b, s, d = x.shape
    depth = w_in.shape[0]
    assert d == D_MODEL and s % ATTN_TILE == 0
    tm = min(TOKEN_TILE, s)
    cos, sin = _rope_tables(s)
    x2 = x.reshape(b * s, d)
    for l in range(depth):
        lambda_init = 0.8 - 0.6 * math.exp(-0.3 * l)
        proj = _in_proj(x2, g_mix[l][None], w_in[l].astype(BF16), cos, sin, s, tm)
        proj3 = proj.reshape(b, s, IN_WIDTH)
        o_sb = _sb_attention(proj3)
        lam_params = jnp.stack([lambda_q1[l], lambda_k1[l], lambda_q2[l], lambda_k2[l]]).astype(F32)
        o_da = _da_attention(proj3, lam_params, g_subln[l][None], lambda_init)
        x2 = _mix_ffn(x2, o_sb.reshape(b * s, SB_WIDTH), o_da.reshape(b * s, DA_V_WIDTH), proj,
                      w_branch_sb[l].astype(BF16), w_branch_da[l].astype(BF16), w_out[l].astype(BF16),
                      g_ffn[l][None], w_ffn_gate[l].astype(BF16), w_ffn_up[l].astype(BF16),
                      w_ffn_down[l].astype(BF16), g_final[None], l == depth - 1, tm)
    return x2.reshape(b, s, d)
```

```python
import functools
import math

import jax
import jax.numpy as jnp
from jax import lax
from jax.experimental import pallas as pl
from jax.experimental.pallas import tpu as pltpu

D_MODEL = 1024
HEAD_DIM = 64
CHUNK = 64
SB_WIDTH = 512
DA_QK_WIDTH = 512
DA_V_WIDTH = 512
DA_HEADS = 4
N_BRANCH = 2
IN_WIDTH = 3 * SB_WIDTH + 2 * DA_QK_WIDTH + DA_V_WIDTH + N_BRANCH * D_MODEL
D_FF = 2816
ROPE_THETA = 10000.0
NORM_EPS = 1e-6
SUBLN_EPS = 1e-5
QK_SCALE = HEAD_DIM ** -0.5
LOG2E = math.log2(math.e)
EXP2_CLAMP = 126.0

LANES = 128
COL_TILE = 512
TOKEN_TILE = 512
ATTN_TILE = 256
SB_PHASE = 4
DA_PHASE = 4
TRIP_PHASES = 10
VMEM_LIMIT = 56 * 1024 * 1024

_QA, _KA, _VA = 0, 4, 8
_QD, _KD, _VD = 12, 16, 20
_GATE = 3

F32 = jnp.float32
BF16 = jnp.bfloat16


def _rms(x, g, eps):
    return x * lax.rsqrt(jnp.mean(x * x, axis=-1, keepdims=True) + eps) * g


def _nt_dot(a, b):
    return lax.dot_general(a, b, (((1,), (1,)), ((), ())), preferred_element_type=F32)


def _dot(a, b):
    return jnp.dot(a, b, preferred_element_type=F32)


def _in_proj_kernel(x_ref, g_ref, w_ref, cos_ref, sin_ref, o_ref):
    h = _rms(x_ref[...], g_ref[...], NORM_EPS).astype(BF16)
    tm = h.shape[0]
    lane = lax.broadcasted_iota(jnp.int32, (tm, LANES), 1)
    first_half = (lane % HEAD_DIM) < (HEAD_DIM // 2)
    cos = cos_ref[...]
    sin = sin_ref[...]

    def rope(a):
        outs = []
        for c in range(a.shape[1] // LANES):
            xs = a[:, c * LANES:(c + 1) * LANES]
            rot = jnp.where(first_half, pltpu.roll(xs, LANES - HEAD_DIM // 2, 1),
                            pltpu.roll(xs, HEAD_DIM // 2, 1))
            outs.append(xs * cos + rot * sin)
        return jnp.concatenate(outs, axis=1)

    for j in range(IN_WIDTH // COL_TILE):
        cols = slice(j * COL_TILE, (j + 1) * COL_TILE)
        a = _dot(h, w_ref[:, cols])
        if j == 0:
            a = a * (QK_SCALE * LOG2E)
        elif j == 3:
            a = rope(a) * (QK_SCALE * LOG2E)
        elif j == 4:
            a = rope(a)
        elif j >= 6:
            a = jax.nn.sigmoid(a)
        o_ref[:, cols] = a.astype(o_ref.dtype)


def _in_proj(x2, g, w, cos, sin, seq, tm):
    t = x2.shape[0]
    pos_blocks = seq // tm
    return pl.pallas_call(
        _in_proj_kernel,
        grid=(t // tm,),
        in_specs=[
            pl.BlockSpec((tm, D_MODEL), lambda i: (i, 0)),
            pl.BlockSpec((1, D_MODEL), lambda i: (0, 0)),
            pl.BlockSpec((D_MODEL, IN_WIDTH), lambda i: (0, 0)),
            pl.BlockSpec((tm, LANES), lambda i: (i % pos_blocks, 0)),
            pl.BlockSpec((tm, LANES), lambda i: (i % pos_blocks, 0)),
        ],
        out_specs=pl.BlockSpec((tm, IN_WIDTH), lambda i: (i, 0)),
        out_shape=jax.ShapeDtypeStruct((t, IN_WIDTH), BF16),
        compiler_params=pltpu.CompilerParams(
            dimension_semantics=("arbitrary",), vmem_limit_bytes=VMEM_LIMIT),
        name="in_proj",
    )(x2, g, w, cos, sin)


def _tile_schedule(n_tiles):
    steps = [(i, i) for i in range(n_tiles)]
    steps += [(i, j) for i in range(n_tiles) for j in range(i - 1, -1, -1)]
    return steps


def _sb_kernel(qi_tab, kj_tab, q_ref, k_ref, v_ref, o_ref, qs, vm, u2, racc, oacc,
               z_sl, hl_sl, rs_sl, w_sl, sc_sl, *, n_steps, n_diag):
    t = ATTN_TILE
    s_len = q_ref.shape[1]
    lane = lax.broadcasted_iota(jnp.int32, (t, LANES), 1)
    head0 = lane < HEAD_DIM

    def prep(c, carry):
        rows = pl.ds(pl.multiple_of(c * t, t), t)
        qc = q_ref[0, rows, :]
        vc = v_ref[0, rows, :]
        zero = jnp.zeros_like(qc)
        qs[0, rows, :] = jnp.where(head0, qc, zero)
        qs[1, rows, :] = jnp.where(head0, zero, qc)
        vm[0, rows, :] = jnp.where(head0, vc, zero)
        vm[1, rows, :] = jnp.where(head0, zero, vc)
        oacc[rows, :] = jnp.zeros((t, LANES), F32)
        racc[0, rows, :] = jnp.zeros((t, LANES), F32)
        racc[1, rows, :] = jnp.zeros((t, LANES), F32)
        return carry

    lax.fori_loop(0, s_len // t, prep, 0)
    row = lax.broadcasted_iota(jnp.int32, (t, t), 0)
    col = lax.broadcasted_iota(jnp.int32, (t, t), 1)
    u2[...] = jnp.where(row > col, 1.0, 0.0).astype(BF16)

    def rows_of(tab, s):
        return pl.ds(pl.multiple_of(tab[s] * t, t), t)

    def stage_a(s, z_s, hl, rs_s, masked):
        qrows = rows_of(qi_tab, s)
        q2 = jnp.concatenate([qs[0, qrows, :], qs[1, qrows, :]], axis=0)
        z = _nt_dot(q2, k_ref[0, rows_of(kj_tab, s), :])
        sp = jnp.maximum(z, jnp.log2(1.0 + jnp.exp2(jnp.minimum(z, EXP2_CLAMP))))
        own = z - sp
        if masked:
            r2 = lax.broadcasted_iota(jnp.int32, (2 * t, t), 0)
            c2 = lax.broadcasted_iota(jnp.int32, (2 * t, t), 1)
            earlier = c2 < (r2 % t)
            sp = jnp.where(earlier, sp, 0.0)
            own = jnp.where(earlier, own, -jnp.inf)
        z_s[...] = own
        hl[...] = sp.astype(BF16)
        rs_s[...] = jnp.broadcast_to(jnp.sum(sp, axis=1, keepdims=True), (2 * t, LANES))

    def stage_b(s, z_s, hl, rs_s, w, sc):
        wv = jnp.exp2(z_s[...] - _dot(hl[...], u2[...])).astype(BF16)
        w[:, 0:t] = wv[:t]
        w[:, t:2 * t] = wv[t:]
        qrows = rows_of(qi_tab, s)
        rs = rs_s[...]
        r0 = racc[0, qrows, :]
        r1 = racc[1, qrows, :]
        racc[0, qrows, :] = r0 + rs[:t]
        racc[1, qrows, :] = r1 + rs[t:]
        sc[...] = jnp.exp2(-jnp.where(head0, r0, r1))

    def stage_c(s, w, sc):
        krows = rows_of(kj_tab, s)
        v2 = jnp.concatenate([vm[0, krows, :], vm[1, krows, :]], axis=0)
        oacc[rows_of(qi_tab, s), :] += _dot(w[...], v2) * sc[...]

    ph = SB_PHASE

    def phase(base, a_set, masked, do_a=True, do_b=True, do_c=True):
        b_set = 1 - a_set
        for u in range(ph):
            if do_a:
                stage_a(base + u, z_sl.at[a_set, u], hl_sl.at[a_set, u], rs_sl.at[a_set, u], masked)
            if do_b:
                stage_b(base - ph + u, z_sl.at[b_set, u], hl_sl.at[b_set, u], rs_sl.at[b_set, u],
                        w_sl.at[b_set, u], sc_sl.at[b_set, u])
            if do_c:
                stage_c(base - 2 * ph + u, w_sl.at[a_set, u], sc_sl.at[a_set, u])

    def trip(j, carry):
        for q in range(TRIP_PHASES):
            phase(n_diag + (TRIP_PHASES * j + q) * ph, q % 2, False)
        return carry

    n_ph = n_steps // ph
    assert n_diag == 4 * ph and (n_ph - 4) % TRIP_PHASES == 0 and TRIP_PHASES % 2 == 0
    phase(0, 0, True, do_b=False, do_c=False)
    phase(ph, 1, True, do_c=False)
    phase(2 * ph, 0, True)
    phase(3 * ph, 1, True)
    lax.fori_loop(0, (n_ph - 4) // TRIP_PHASES, trip, 0)
    phase(n_steps, 0, False, do_a=False)
    phase(n_steps + ph, 1, False, do_a=False, do_b=False)
    o_ref[0] = oacc[...].astype(o_ref.dtype)


def _sb_attention(proj3):
    b, s, _ = proj3.shape
    t = ATTN_TILE
    n_tiles = s // t
    steps = _tile_schedule(n_tiles)
    assert n_tiles % (2 * SB_PHASE) == 0 and len(steps) % (2 * SB_PHASE) == 0
    qi_tab = jnp.asarray([q for q, _ in steps], jnp.int32)
    kj_tab = jnp.asarray([k for _, k in steps], jnp.int32)
    col_block = lambda off: pl.BlockSpec((1, s, LANES), lambda bi, p, *_: (bi, 0, off + p))
    return pl.pallas_call(
        functools.partial(_sb_kernel, n_steps=len(steps), n_diag=n_tiles),
        grid_spec=pltpu.PrefetchScalarGridSpec(
            num_scalar_prefetch=2,
            grid=(b, SB_WIDTH // LANES),
            in_specs=[col_block(_QA), col_block(_KA), col_block(_VA)],
            out_specs=pl.BlockSpec((1, s, LANES), lambda bi, p, *_: (bi, 0, p)),
            scratch_shapes=[
                pltpu.VMEM((2, s, LANES), BF16),
                pltpu.VMEM((2, s, LANES), BF16),
                pltpu.VMEM((t, t), BF16),
                pltpu.VMEM((2, s, LANES), F32),
                pltpu.VMEM((s, LANES), F32),
                pltpu.VMEM((2, SB_PHASE, 2 * t, t), F32),
                pltpu.VMEM((2, SB_PHASE, 2 * t, t), BF16),
                pltpu.VMEM((2, SB_PHASE, 2 * t, LANES), F32),
                pltpu.VMEM((2, SB_PHASE, t, 2 * t), BF16),
                pltpu.VMEM((2, SB_PHASE, t, LANES), F32),
            ]),
        out_shape=jax.ShapeDtypeStruct((b, s, SB_WIDTH), BF16),
        compiler_params=pltpu.CompilerParams(
            dimension_semantics=("arbitrary", "arbitrary"), vmem_limit_bytes=VMEM_LIMIT),
        name="sb_attn",
    )(qi_tab, kj_tab, proj3, proj3, proj3)


def _da_kernel(qi_tab, kj_tab, lam_ref, g_ref, q_ref, k_ref, v_ref, o_ref, qs, vx, acc, macc,
               p_sl, al_sl, *, lambda_init, n_steps, n_diag):
    t = ATTN_TILE
    s_len = q_ref.shape[1]
    lane = lax.broadcasted_iota(jnp.int32, (t, LANES), 1)
    map0 = lane < HEAD_DIM

    def prep(c, carry):
        rows = pl.ds(pl.multiple_of(c * t, t), t)
        qc = q_ref[0, rows, :]
        zero = jnp.zeros_like(qc)
        qs[0, rows, :] = jnp.where(map0, qc, zero)
        qs[1, rows, :] = jnp.where(map0, zero, qc)
        vx[rows, 0:LANES] = v_ref[0, rows, :]
        vx[rows, LANES:2 * LANES] = jnp.ones((t, LANES), BF16)
        for c2 in range(2):
            acc[c2, rows, :] = jnp.zeros((t, 2 * LANES), F32)
            macc[c2, rows, :] = jnp.full((t, LANES), -jnp.inf, F32)
        return carry

    lax.fori_loop(0, s_len // t, prep, 0)

    def rows_of(tab, s):
        return pl.ds(pl.multiple_of(tab[s] * t, t), t)

    def stage_a(s, p_s, al_s, masked):
        qrows = rows_of(qi_tab, s)
        q2 = jnp.concatenate([qs[0, qrows, :], qs[1, qrows, :]], axis=0)
        sc = _nt_dot(q2, k_ref[0, rows_of(kj_tab, s), :])
        if masked:
            r2 = lax.broadcasted_iota(jnp.int32, (2 * t, t), 0)
            c2 = lax.broadcasted_iota(jnp.int32, (2 * t, t), 1)
            sc = jnp.where((c2 // CHUNK) <= ((r2 % t) // CHUNK), sc, -jnp.inf)
        m_prev = jnp.concatenate([macc[0, qrows, :], macc[1, qrows, :]], axis=0)
        m_new = jnp.maximum(m_prev, jnp.max(sc, axis=1, keepdims=True))
        p_s[...] = jnp.exp2(sc - jnp.concatenate([m_new] * (t // LANES), axis=1)).astype(BF16)
        al_s[...] = jnp.exp2(m_prev - m_new)
        for c in range(2):
            macc[c, qrows, :] = m_new[c * t:(c + 1) * t]

    def stage_b(s, p_s, al_s):
        qrows = rows_of(qi_tab, s)
        pv = _dot(p_s[...], vx[rows_of(kj_tab, s), :])
        al = al_s[...]
        al = jnp.concatenate([al, al], axis=1)
        for c in range(2):
            acc[c, qrows, :] = al[c * t:(c + 1) * t] * acc[c, qrows, :] + pv[c * t:(c + 1) * t]

    ph = DA_PHASE

    def phase(base, a_set, masked, do_a=True, do_b=True):
        b_set = 1 - a_set
        for u in range(ph):
            if do_a:
                stage_a(base + u, p_sl.at[a_set, u], al_sl.at[a_set, u], masked)
            if do_b:
                stage_b(base - ph + u, p_sl.at[b_set, u], al_sl.at[b_set, u])

    def trip(j, carry):
        for q in range(TRIP_PHASES):
            phase(n_diag + (TRIP_PHASES * j + q) * ph, q % 2, False)
        return carry

    n_ph = n_steps // ph
    assert n_diag == 4 * ph and (n_ph - 4) % TRIP_PHASES == 0 and TRIP_PHASES % 2 == 0
    phase(0, 0, True, do_b=False)
    phase(ph, 1, True)
    phase(2 * ph, 0, True)
    phase(3 * ph, 1, True)
    lax.fori_loop(0, (n_ph - 4) // TRIP_PHASES, trip, 0)
    phase(n_steps, 0, False, do_a=False)

    lam_p = lam_ref[...]
    lam = (jnp.exp(jnp.sum(lam_p[0:1] * lam_p[1:2], axis=1, keepdims=True))
           - jnp.exp(jnp.sum(lam_p[2:3] * lam_p[3:4], axis=1, keepdims=True)) + lambda_init)
    g = g_ref[...]

    def finish(c, carry):
        rows = pl.ds(pl.multiple_of(c * t, t), t)
        a0 = acc[0, rows, :]
        a1 = acc[1, rows, :]
        o = a0[:, :LANES] / a0[:, LANES:] - lam * (a1[:, :LANES] / a1[:, LANES:])
        o_ref[0, rows, :] = (_rms(o, g, SUBLN_EPS) * (1.0 - lambda_init)).astype(o_ref.dtype)
        return carry

    lax.fori_loop(0, s_len // t, finish, 0)


def _da_attention(proj3, lam_params, g_subln, lambda_init):
    b, s, _ = proj3.shape
    t = ATTN_TILE
    n_tiles = s // t
    steps = _tile_schedule(n_tiles)
    assert n_tiles == 4 * DA_PHASE
    qi_tab = jnp.asarray([q for q, _ in steps], jnp.int32)
    kj_tab = jnp.asarray([k for _, k in steps], jnp.int32)
    col_block = lambda off: pl.BlockSpec((1, s, LANES), lambda bi, h, *_: (bi, 0, off + h))
    return pl.pallas_call(
        functools.partial(_da_kernel, lambda_init=lambda_init, n_steps=len(steps), n_diag=n_tiles),
        grid_spec=pltpu.PrefetchScalarGridSpec(
            num_scalar_prefetch=2,
            grid=(b, DA_HEADS),
            in_specs=[
                pl.BlockSpec((4, HEAD_DIM), lambda bi, h, *_: (0, 0)),
                pl.BlockSpec((1, LANES), lambda bi, h, *_: (0, 0)),
                col_block(_QD), col_block(_KD), col_block(_VD),
            ],
            out_specs=pl.BlockSpec((1, s, LANES), lambda bi, h, *_: (bi, 0, h)),
            scratch_shapes=[
                pltpu.VMEM((2, s, LANES), BF16),
                pltpu.VMEM((s, 2 * LANES), BF16),
                pltpu.VMEM((2, s, 2 * LANES), F32),
                pltpu.VMEM((2, s, LANES), F32),
                pltpu.VMEM((2, DA_PHASE, 2 * t, t), BF16),
                pltpu.VMEM((2, DA_PHASE, 2 * t, LANES), F32),
            ]),
        out_shape=jax.ShapeDtypeStruct((b, s, DA_V_WIDTH), BF16),
        compiler_params=pltpu.CompilerParams(
            dimension_semantics=("arbitrary", "arbitrary"), vmem_limit_bytes=VMEM_LIMIT),
        name="da_attn",
    )(qi_tab, kj_tab, lam_params, g_subln, proj3, proj3, proj3)


def _mix_ffn_kernel(x_ref, osb_ref, oda_ref, gsb_ref, gda_ref, wsb_ref, wda_ref, wout_ref,
                    g_ref, wg_ref, wu_ref, wd_ref, gf_ref, o_ref, *, final_norm):
    mixed = (gsb_ref[...].astype(F32) * _dot(osb_ref[...], wsb_ref[...])
             + gda_ref[...].astype(F32) * _dot(oda_ref[...], wda_ref[...]))
    x = x_ref[...] + _dot(mixed.astype(BF16), wout_ref[...])
    h = _rms(x, g_ref[...], NORM_EPS).astype(BF16)
    a = (jax.nn.silu(_dot(h, wg_ref[...])) * _dot(h, wu_ref[...])).astype(BF16)
    y = x + _dot(a, wd_ref[...])
    if final_norm:
        y = _rms(y, gf_ref[...], NORM_EPS)
    o_ref[...] = y


def _mix_ffn(x2, o_sb, o_da, proj, w_sb, w_da, w_out, g, w_gate, w_up, w_down, g_final, final_norm, tm):
    t = x2.shape[0]
    rows = lambda width, col=0: pl.BlockSpec((tm, width), lambda i: (i, col))
    whole = lambda shape: pl.BlockSpec(shape, lambda i: (0, 0), pipeline_mode=pl.Buffered(1))
    return pl.pallas_call(
        functools.partial(_mix_ffn_kernel, final_norm=final_norm),
        grid=(t // tm,),
        in_specs=[
            rows(D_MODEL), rows(SB_WIDTH), rows(DA_V_WIDTH), rows(D_MODEL, _GATE), rows(D_MODEL, _GATE + 1),
            whole((SB_WIDTH, D_MODEL)), whole((DA_V_WIDTH, D_MODEL)), whole((D_MODEL, D_MODEL)),
            whole((1, D_MODEL)), whole((D_MODEL, D_FF)), whole((D_MODEL, D_FF)), whole((D_FF, D_MODEL)),
            whole((1, D_MODEL)),
        ],
        out_specs=rows(D_MODEL),
        out_shape=jax.ShapeDtypeStruct((t, D_MODEL), F32),
        compiler_params=pltpu.CompilerParams(
            dimension_semantics=("arbitrary",), vmem_limit_bytes=VMEM_LIMIT),
        name="mix_ffn",
    )(x2, o_sb, o_da, proj, proj, w_sb, w_da, w_out, g, w_gate, w_up, w_down, g_final)


def _rope_tables(seq_len):
    inv_freq = 1.0 / (ROPE_THETA ** (jnp.arange(0, HEAD_DIM, 2, dtype=F32) / HEAD_DIM))
    ang = jnp.arange(seq_len, dtype=F32)[:, None] * inv_freq[None, :]
    ang = jnp.concatenate([ang, ang], axis=-1)
    sign = jnp.where(jnp.arange(HEAD_DIM) < HEAD_DIM // 2, -1.0, 1.0).astype(F32)
    cos = jnp.tile(jnp.cos(ang), (1, LANES // HEAD_DIM))
    sin = jnp.tile(jnp.sin(ang) * sign[None, :], (1, LANES // HEAD_DIM))
    return cos, sin


def kernel(x, g_mix, w_in, lambda_q1, lambda_k1, lambda_q2, lambda_k2, g_subln, w_branch_sb, w_branch_da,
           w_out, g_ffn, w_ffn_gate, w_ffn_up, w_ffn_down, g_final):
    b, s, d = x.shape
    depth = w_in.shape[0]
    assert d == D_MODEL and s % ATTN_TILE == 0
    tm = min(TOKEN_TILE, s)
    cos, sin = _rope_tables(s)
    x2 = x.reshape(b * s, d)
    for l in range(depth):
        lambda_init = 0.8 - 0.6 * math.exp(-0.3 * l)
        proj = _in_proj(x2, g_mix[l][None], w_in[l].astype(BF16), cos, sin, s, tm)
        proj3 = proj.reshape(b, s, IN_WIDTH)
        o_sb = _sb_attention(proj3)
        lam_params = jnp.stack([lambda_q1[l], lambda_k1[l], lambda_q2[l], lambda_k2[l]]).astype(F32)
        o_da = _da_attention(proj3, lam_params, g_subln[l][None], lambda_init)
        x2 = _mix_ffn(x2, o_sb.reshape(b * s, SB_WIDTH), o_da.reshape(b * s, DA_V_WIDTH), proj,
                      w_branch_sb[l].astype(BF16), w_branch_da[l].astype(BF16), w_out[l].astype(BF16),
                      g_ffn[l][None], w_ffn_gate[l].astype(BF16), w_ffn_up[l].astype(BF16),
                      w_ffn_down[l].astype(BF16), g_final[None], l == depth - 1, tm)
    return x2.reshape(b, s, d)
```
